```python
import jax, jax.numpy as jnp
from jax import lax
import numpy as np

D_MODEL = 1024
BATCH = 8
SEQ = 4096
DEPTH = 2
DEC_BATCH = 32
DEC_SEQ = 64
PAST_LEN = 1024

CHUNK = 64
EPS = 1e-6
W_A = D_MODEL
K_A = 3
W_B = D_MODEL
K_B = 31
W_C = D_MODEL
POOL_WINDOWS = (2, 4, 8, 16)
N_POOL_GROUPS = 4
POOL_GROUP = W_C // N_POOL_GROUPS
POOL_PAD = max(POOL_WINDOWS) - 1
N_BRANCH = 3
SPLITS = (W_A, 2 * W_A, 3 * W_A, 3 * W_A + W_B, 3 * W_A + 2 * W_B, 3 * W_A + 2 * W_B + W_C)
N_IN = 3 * W_A + 2 * W_B + W_C + N_BRANCH * D_MODEL
D_FF = -(-8 * D_MODEL // (3 * 256)) * 256

kernel_name = "hybrid_streaming_conv_pool_encoder_step"


def rms_norm(x, g):
    xf = x.astype(jnp.float32)
    y = xf * lax.rsqrt(jnp.mean(xf * xf, axis=-1, keepdims=True) + EPS)
    return (y * g.astype(jnp.float32)).astype(x.dtype)


def layer_norm(x, g, b):
    xf = x.astype(jnp.float32)
    mu = jnp.mean(xf, axis=-1, keepdims=True)
    var = jnp.mean(jnp.square(xf - mu), axis=-1, keepdims=True)
    y = (xf - mu) * lax.rsqrt(var + EPS)
    return (y * g.astype(jnp.float32) + b.astype(jnp.float32)).astype(x.dtype)


def causal_depthwise(full, w):
    c = w.shape[1]
    return lax.conv_general_dilated(full, w[:, None, :], window_strides=(1,), padding='VALID',
                                    dimension_numbers=('NWC', 'WIO', 'NWC'), feature_group_count=c)


def multiscale_pool(full, pos0):
    s = full.shape[1] - POOL_PAD
    f = full.astype(jnp.float32)
    cs = jnp.cumsum(f, axis=1)
    cs = jnp.concatenate([jnp.zeros_like(cs[:, :1]), cs], axis=1)
    end = POOL_PAD + 1
    xcur = f[:, POOL_PAD:]
    pos = pos0 + jnp.arange(s)
    outs = []
    for g, w in enumerate(POOL_WINDOWS):
        sl = slice(g * POOL_GROUP, (g + 1) * POOL_GROUP)
        wsum = cs[:, end:end + s, sl] - cs[:, end - w:end - w + s, sl]
        cnt = jnp.minimum(pos + 1, w).astype(jnp.float32)[None, :, None]
        outs.append(wsum / cnt - xcur[:, :, sl])
    return jnp.stack(outs, axis=2).astype(full.dtype)


def mixer_branches(h, prev_a, prev_b, prev_p, pos0, w_in, w_conv_a, w_out_a, w_conv_b, b_conv_b,
                   ln_b_g, ln_b_b, w_out_b, w_pool, pool_scale, w_o):
    bsz, s, _ = h.shape
    z = jnp.einsum('bsd,dn->bsn', h, w_in)
    bg, cg, ha, ga, gb, pin, gates = jnp.split(z, SPLITS, axis=-1)
    u = cg * ha
    full_a = jnp.concatenate([prev_a, u], axis=1)
    y_a = jnp.einsum('bsc,cd->bsd', bg * causal_depthwise(full_a, w_conv_a), w_out_a)
    v = ga * jax.nn.sigmoid(gb)
    full_b = jnp.concatenate([prev_b, v], axis=1)
    cb = causal_depthwise(full_b, w_conv_b) + b_conv_b
    y_b = jnp.einsum('bsc,cd->bsd', jax.nn.silu(layer_norm(cb, ln_b_g, ln_b_b)), w_out_b)
    full_p = jnp.concatenate([prev_p, pin], axis=1)
    pooled = multiscale_pool(full_p, pos0)
    y_c = jnp.einsum('bsgc,gce->bsge', pooled, w_pool).reshape(bsz, s, W_C) * pool_scale
    gt = jax.nn.sigmoid(gates).reshape(bsz, s, N_BRANCH, D_MODEL)
    m = gt[:, :, 0] * y_a + gt[:, :, 1] * y_b + gt[:, :, 2] * y_c
    out = jnp.einsum('bsd,de->bse', m, w_o)
    return (out, full_a[:, full_a.shape[1] - (K_A - 1):], full_b[:, full_b.shape[1] - (K_B - 1):],
            full_p[:, full_p.shape[1] - POOL_PAD:])


def run_trunk(x, c, prev_a, prev_b, prev_p, pos0, w_ada, b_ada, norm1_g, w_in, w_conv_a, w_out_a,
              w_conv_b, b_conv_b, ln_b_g, ln_b_b, w_out_b, w_pool, pool_scale, w_o, norm2_g,
              w_ffn_in, w_ffn_out, final_g):
    new_a, new_b, new_p = [], [], []
    sc = jax.nn.silu(c)
    for l in range(DEPTH):
        ada = jnp.einsum('bd,de->be', sc, w_ada[l]) + b_ada[l]
        sh1, s1, g1, sh2, s2, g2 = [t[:, None, :] for t in jnp.split(ada, 6, axis=-1)]
        h = rms_norm(x, norm1_g[l]) * (1 + s1) + sh1
        out, na, nb, npl = mixer_branches(h, prev_a[l], prev_b[l], prev_p[l], pos0, w_in[l], w_conv_a[l],
                                          w_out_a[l], w_conv_b[l], b_conv_b[l], ln_b_g[l], ln_b_b[l],
                                          w_out_b[l], w_pool[l], pool_scale[l], w_o[l])
        x = x + g1 * out
        h2 = rms_norm(x, norm2_g[l]) * (1 + s2) + sh2
        gate, up = jnp.split(jnp.einsum('bsd,df->bsf', h2, w_ffn_in[l]), 2, axis=-1)
        x = x + g2 * jnp.einsum('bsf,fd->bsd', jax.nn.silu(gate) * up, w_ffn_out[l])
        new_a.append(na); new_b.append(nb); new_p.append(npl)
    return rms_norm(x, final_g), jnp.stack(new_a), jnp.stack(new_b), jnp.stack(new_p)


def setup_inputs(seed: int = 0) -> dict:
    key = jax.random.key(seed)
    ks = jax.random.split(key, 26)
    L = DEPTH

    def n(k, shape, s):
        return jax.random.normal(k, shape, jnp.float32) * s

    return {
        "x_prompt": n(ks[0], (BATCH, SEQ, D_MODEL), 1.0),
        "x_sample": n(ks[1], (DEC_BATCH, DEC_SEQ, D_MODEL), 1.0),
        "c_prompt": n(ks[2], (BATCH, D_MODEL), 1.0),
        "c_sample": n(ks[3], (DEC_BATCH, D_MODEL), 1.0),
        "cache_conv_a": n(ks[4], (L, DEC_BATCH, K_A - 1, W_A), 1.0),
        "cache_conv_b": n(ks[5], (L, DEC_BATCH, K_B - 1, W_B), 0.5),
        "cache_pool": n(ks[6], (L, DEC_BATCH, POOL_PAD, W_C), 1.0),
        "w_ada": n(ks[7], (L, D_MODEL, 6 * D_MODEL), D_MODEL ** -0.5),
        "b_ada": n(ks[8], (L, 6 * D_MODEL), 0.01),
        "norm1_g": 1.0 + n(ks[9], (L, D_MODEL), 0.02),
        "w_in": n(ks[10], (L, D_MODEL, N_IN), D_MODEL ** -0.5),
        "w_conv_a": n(ks[11], (L, K_A, W_A), K_A ** -0.5),
        "w_out_a": n(ks[12], (L, W_A, D_MODEL), W_A ** -0.5),
        "w_conv_b": n(ks[13], (L, K_B, W_B), K_B ** -0.5),
        "b_conv_b": n(ks[14], (L, W_B), 0.01),
        "ln_b_g": 1.0 + n(ks[15], (L, W_B), 0.02),
        "ln_b_b": n(ks[16], (L, W_B), 0.01),
        "w_out_b": n(ks[17], (L, W_B, D_MODEL), W_B ** -0.5),
        "w_pool": n(ks[18], (L, N_POOL_GROUPS, POOL_GROUP, POOL_GROUP), POOL_GROUP ** -0.5),
        "pool_scale": 1.0 + n(ks[19], (L, W_C), 0.02),
        "w_o": n(ks[20], (L, D_MODEL, D_MODEL), D_MODEL ** -0.5),
        "norm2_g": 1.0 + n(ks[21], (L, D_MODEL), 0.02),
        "w_ffn_in": n(ks[22], (L, D_MODEL, 2 * D_FF), D_MODEL ** -0.5),
        "w_ffn_out": n(ks[23], (L, D_FF, D_MODEL), D_FF ** -0.5),
        "final_g": 1.0 + n(ks[24], (D_MODEL,), 0.02),
    }


def reference(x_prompt, x_sample, c_prompt, c_sample, cache_conv_a, cache_conv_b, cache_pool,
              w_ada, b_ada, norm1_g, w_in, w_conv_a, w_out_a, w_conv_b, b_conv_b, ln_b_g, ln_b_b,
              w_out_b, w_pool, pool_scale, w_o, norm2_g, w_ffn_in, w_ffn_out, final_g):
    bp = x_prompt.shape[0]
    dt = x_prompt.dtype
    zero_a = jnp.zeros((DEPTH, bp, K_A - 1, W_A), dt)
    zero_b = jnp.zeros((DEPTH, bp, K_B - 1, W_B), dt)
    zero_p = jnp.zeros((DEPTH, bp, POOL_PAD, W_C), dt)
    y_prompt, state_conv_a_prompt, state_conv_b_prompt, state_pool_prompt = run_trunk(
        x_prompt, c_prompt, zero_a, zero_b, zero_p, 0, w_ada, b_ada, norm1_g, w_in, w_conv_a, w_out_a,
        w_conv_b, b_conv_b, ln_b_g, ln_b_b, w_out_b, w_pool, pool_scale, w_o, norm2_g,
        w_ffn_in, w_ffn_out, final_g)
    y_sample, state_conv_a_sample, state_conv_b_sample, state_pool_sample = run_trunk(
        x_sample, c_sample, cache_conv_a, cache_conv_b, cache_pool, PAST_LEN, w_ada, b_ada, norm1_g, w_in,
        w_conv_a, w_out_a, w_conv_b, b_conv_b, ln_b_g, ln_b_b, w_out_b, w_pool, pool_scale, w_o, norm2_g,
        w_ffn_in, w_ffn_out, final_g)
    return (y_prompt, y_sample, state_conv_a_prompt, state_conv_b_prompt, state_pool_prompt,
            state_conv_a_sample, state_conv_b_sample, state_pool_sample)
```

```python
import functools

import jax
import jax.numpy as jnp
from jax import lax
from jax.experimental import pallas as pl
from jax.experimental.pallas import tpu as pltpu

F32 = jnp.float32
BF16 = jnp.bfloat16

D_MODEL = 1024
DEPTH = 2
EPS = 1e-6
K_A = 3
K_B = 31
POOL_WINDOWS = (2, 4, 8, 16)
POOL_GROUP = D_MODEL // len(POOL_WINDOWS)
POOL_PAD = max(POOL_WINDOWS) - 1
N_IN = 9 * D_MODEL
D_FF = 2816
PAST_LEN = 1024

SUBLANES = 8
LANES = 128
ROWS = 16
N_CHUNKS = D_MODEL // LANES
VMEM_LIMIT_BYTES = 60 * 1024 * 1024

C_BG, C_CG, C_HA, C_GA, C_GB, C_PIN, C_GATES = (i * D_MODEL for i in range(7))

A_SH1, A_S1, A_G1, A_SH2, A_S2, A_G2 = range(6)


def _sigmoid(x):
    return 1.0 / (1.0 + jnp.exp(-x))


def _silu(x):
    return x * _sigmoid(x)


def _dot(a, b):
    return jnp.dot(a, b, preferred_element_type=F32)


def _rows(i, n=ROWS):
    return pl.ds(pl.multiple_of(i * n, n), n)


def _rms(x):
    return x * lax.rsqrt(jnp.mean(x * x, axis=-1, keepdims=True) + EPS)


def _ada_kernel(c_ref, w_ref, b_ref, o_ref):
    sc = _silu(c_ref[...]).astype(BF16)
    o_ref[...] = _dot(sc, w_ref[...].astype(BF16)) + b_ref[...]


def _ada_call(c_all, w_ada, b_ada):
    n = c_all.shape[0]
    tn = 1536
    return pl.pallas_call(
        _ada_kernel,
        grid=(DEPTH, 6 * D_MODEL // tn),
        in_specs=[
            pl.BlockSpec((n, D_MODEL), lambda l, j: (0, 0)),
            pl.BlockSpec((None, D_MODEL, tn), lambda l, j: (l, 0, j)),
            pl.BlockSpec((None, 1, tn), lambda l, j: (l, 0, j)),
        ],
        out_specs=pl.BlockSpec((None, n, tn), lambda l, j: (l, 0, j)),
        out_shape=jax.ShapeDtypeStruct((DEPTH, n, 6 * D_MODEL), F32),
        compiler_params=pltpu.CompilerParams(
            dimension_semantics=("arbitrary", "arbitrary"),
            vmem_limit_bytes=VMEM_LIMIT_BYTES),
        name="ada",
    )(c_all, w_ada, b_ada.reshape(DEPTH, 1, 6 * D_MODEL))


def _mixer_kernel(*refs, q_frames, sps, chained, permute_in, pos0):
    Q = q_frames
    T = SUBLANES * Q
    refs = list(refs)
    x_ref, ada_ref = refs[:2]
    refs = refs[2:]
    if not chained:
        hist_a_ref, hist_b_ref, hist_p_ref = refs[:3]
        refs = refs[3:]
    (n1g_ref, win_ref, wca_ref, woa_ref, wcb_ref, bcb_ref, lng_ref, lnb_ref,
     wob_ref, wpool_ref, pscale_ref, wo_ref) = refs[:12]
    refs = refs[12:]
    xo_ref, sa_ref, sb_ref, sp_ref = refs[:4]
    refs = refs[4:]
    (h_ref, z_ref, y_ref, m_ref, eu_ref, ev_ref, ep_ref,
     a_ref, b_ref, c_ref) = refs[:10]
    refs = refs[10:]
    if chained:
        cu_ref, cv_ref, cp_ref = refs[:3]
        refs = refs[3:]
    if permute_in:
        (xs_ref,) = refs
    else:
        xs_ref = x_ref

    j = pl.program_id(1)
    n_blocks = T // ROWS

    if permute_in:
        def perm_body(q, carry):
            for c in range(N_CHUNKS):
                blk = x_ref[pl.ds(q * N_CHUNKS + c, SUBLANES, stride=N_CHUNKS * Q), :]
                xs_ref[_rows(q, SUBLANES), c * LANES:(c + 1) * LANES] = blk
            return carry
        lax.fori_loop(0, Q, perm_body, 0)

    if chained:
        @pl.when(j == 0)
        def _():
            cu_ref[...] = jnp.zeros_like(cu_ref)
            cv_ref[...] = jnp.zeros_like(cv_ref)
            cp_ref[...] = jnp.zeros_like(cp_ref)

    def fill_halo(e_ref, carry_ref, hist_ref, n_halo):
        stream = lax.broadcasted_iota(jnp.int32, (SUBLANES, D_MODEL), 0)
        first = (stream & (sps - 1)) == 0
        for g in range(n_halo):
            dst = slice(SUBLANES * g, SUBLANES * (g + 1))
            tail = e_ref[SUBLANES * (Q + g):SUBLANES * (Q + g + 1), :]
            prev = pltpu.roll(tail, 1, axis=0)
            hist = carry_ref[dst, :] if chained else hist_ref[g]
            e_ref[dst, :] = jnp.where(first, hist, prev)
            if chained:
                carry_ref[dst, :] = prev

    n1g = n1g_ref[...]
    def norm_body(i, carry):
        x = xs_ref[_rows(i), :]
        y = _rms(x) * n1g
        h = y * (1.0 + ada_ref[A_S1]) + ada_ref[A_SH1]
        h_ref[_rows(i), :] = h.astype(BF16)
        return carry
    lax.fori_loop(0, n_blocks, norm_body, 0)

    z_ref[...] = _dot(h_ref[...], win_ref[:, C_BG:C_GA])
    def u_body(i, carry):
        u = z_ref[_rows(i), C_CG:C_HA] * z_ref[_rows(i), C_HA:C_GA]
        eu_ref[pl.ds(pl.multiple_of(i * ROWS + SUBLANES * (K_A - 1), SUBLANES), ROWS), :] = u
        return carry
    lax.fori_loop(0, n_blocks, u_body, 0)
    fill_halo(eu_ref, cu_ref if chained else None, None if chained else hist_a_ref, K_A - 1)
    def conv_a_body(i, carry):
        acc = None
        for k in range(K_A):
            win = eu_ref[pl.ds(pl.multiple_of(i * ROWS + SUBLANES * k, SUBLANES), ROWS), :]
            t = win * wca_ref[k:k + 1, :]
            acc = t if acc is None else acc + t
        a_ref[_rows(i), :] = (z_ref[_rows(i), C_BG:C_CG] * acc).astype(BF16)
        return carry
    lax.fori_loop(0, n_blocks, conv_a_body, 0)

    z_ref[:, 0:2 * D_MODEL] = _dot(h_ref[...], win_ref[:, C_GA:C_PIN])
    def v_body(i, carry):
        v = z_ref[_rows(i), 0:D_MODEL] * _sigmoid(z_ref[_rows(i), D_MODEL:2 * D_MODEL])
        ev_ref[pl.ds(pl.multiple_of(i * ROWS + SUBLANES * (K_B - 1), SUBLANES), ROWS), :] = v
        return carry
    lax.fori_loop(0, n_blocks, v_body, 0)
    fill_halo(ev_ref, cv_ref if chained else None, None if chained else hist_b_ref, K_B - 1)
    conv_rows = 64
    def conv_b_body(c, carry):
        col = pl.ds(pl.multiple_of(c * LANES, LANES), LANES)
        for rb in range(T // conv_rows):
            r0 = rb * conv_rows
            acc = jnp.broadcast_to(bcb_ref[:, col], (conv_rows, LANES))
            for k in range(K_B):
                win = ev_ref[r0 + SUBLANES * k:r0 + SUBLANES * k + conv_rows, col]
                acc = acc + win * wcb_ref[k:k + 1, col]
            y_ref[r0:r0 + conv_rows, col] = acc
        return carry
    lax.fori_loop(0, N_CHUNKS, conv_b_body, 0)
    lng = lng_ref[...]
    lnb = lnb_ref[...]
    def ln_body(i, carry):
        cb = y_ref[_rows(i), :]
        mu = jnp.mean(cb, axis=-1, keepdims=True)
        d = cb - mu
        var = jnp.mean(d * d, axis=-1, keepdims=True)
        y = d * lax.rsqrt(var + EPS) * lng + lnb
        b_ref[_rows(i), :] = _silu(y).astype(BF16)
        return carry
    lax.fori_loop(0, n_blocks, ln_body, 0)

    ep_ref[SUBLANES * POOL_PAD:, :] = _dot(h_ref[...], win_ref[:, C_PIN:C_GATES])
    fill_halo(ep_ref, cp_ref if chained else None, None if chained else hist_p_ref, POOL_PAD)
    def pool_body(i, carry):
        base = i * ROWS + SUBLANES * POOL_PAD
        if pos0 < POOL_PAD:
            rr = lax.broadcasted_iota(jnp.int32, (ROWS, 1), 0)
            frame = j * T + (rr & (SUBLANES - 1)) * Q + (i * (ROWS // SUBLANES) + (rr >> 3))
            avail = pos0 + frame + 1
        outs = []
        for g, w in enumerate(POOL_WINDOWS):
            cols = slice(g * POOL_GROUP, (g + 1) * POOL_GROUP)
            cur = ep_ref[pl.ds(pl.multiple_of(base, SUBLANES), ROWS), cols]
            acc = cur
            for s in range(1, w):
                acc = acc + ep_ref[pl.ds(pl.multiple_of(base - SUBLANES * s, SUBLANES), ROWS), cols]
            if pos0 < POOL_PAD:
                inv = 1.0 / jnp.minimum(avail, w).astype(F32)
            else:
                inv = 1.0 / w
            outs.append(acc * inv - cur)
        c_ref[_rows(i), :] = jnp.concatenate(outs, axis=-1).astype(BF16)
        return carry
    lax.fori_loop(0, n_blocks, pool_body, 0)

    z_ref[...] = _dot(h_ref[...], win_ref[:, C_GATES:N_IN])
    y_ref[...] = _dot(a_ref[...], woa_ref[...])
    def merge_a_body(i, carry):
        m_ref[_rows(i), :] = _sigmoid(z_ref[_rows(i), 0:D_MODEL]) * y_ref[_rows(i), :]
        return carry
    lax.fori_loop(0, n_blocks, merge_a_body, 0)
    y_ref[...] = _dot(b_ref[...], wob_ref[...])
    def merge_b_body(i, carry):
        m_ref[_rows(i), :] += (_sigmoid(z_ref[_rows(i), D_MODEL:2 * D_MODEL])
                               * y_ref[_rows(i), :])
        return carry
    lax.fori_loop(0, n_blocks, merge_b_body, 0)
    for g in range(len(POOL_WINDOWS)):
        cols = slice(g * POOL_GROUP, (g + 1) * POOL_GROUP)
        y_ref[:, cols] = _dot(c_ref[:, cols], wpool_ref[g])
    pscale = pscale_ref[...]
    def merge_c_body(i, carry):
        m = m_ref[_rows(i), :] + (_sigmoid(z_ref[_rows(i), 2 * D_MODEL:3 * D_MODEL])
                                  * (y_ref[_rows(i), :] * pscale))
        h_ref[_rows(i), :] = m.astype(BF16)
        return carry
    lax.fori_loop(0, n_blocks, merge_c_body, 0)

    y_ref[...] = _dot(h_ref[...], wo_ref[...])
    def res_body(i, carry):
        xo_ref[_rows(i), :] = xs_ref[_rows(i), :] + ada_ref[A_G1] * y_ref[_rows(i), :]
        return carry
    lax.fori_loop(0, n_blocks, res_body, 0)

    if chained:
        @pl.when(j == pl.num_programs(1) - 1)
        def _():
            for carry_ref, s_ref, n_halo in ((cu_ref, sa_ref, K_A - 1),
                                             (cv_ref, sb_ref, K_B - 1),
                                             (cp_ref, sp_ref, POOL_PAD)):
                for g in range(n_halo):
                    s_ref[g:g + 1, :] = carry_ref[SUBLANES * g:SUBLANES * g + 1, :]
    else:
        for e_ref, s_ref, n_halo in ((eu_ref, sa_ref, K_A - 1),
                                     (ev_ref, sb_ref, K_B - 1),
                                     (ep_ref, sp_ref, POOL_PAD)):
            for g in range(n_halo):
                s_ref[g] = e_ref[SUBLANES * (Q + g):SUBLANES * (Q + g + 1), :]


def _const_spec(shape):
    nd = len(shape)
    return pl.BlockSpec(shape, lambda *_: (0,) * nd, pipeline_mode=pl.Buffered(1))


def _mixer_call(x, ada, hists, w, *, n_seq, tiles_per_seq, q_frames, sps, chained,
                permute_in, pos0):
    Q = q_frames
    T = SUBLANES * Q
    n_tiles = n_seq * tiles_per_seq
    tile_idx = lambda b, j: (b * tiles_per_seq + j, 0)
    if permute_in:
        x_spec = pl.BlockSpec((T * N_CHUNKS, LANES), tile_idx)
    else:
        x_spec = pl.BlockSpec((T, D_MODEL), tile_idx)
    in_specs = [x_spec, pl.BlockSpec((None, 6, ROWS, D_MODEL), lambda b, j: (b, 0, 0, 0))]
    args = [x, ada]
    if not chained:
        for hst in hists:
            in_specs.append(pl.BlockSpec((None,) + hst.shape[1:], lambda b, j: (b, 0, 0, 0)))
            args.append(hst)
    for arr in w:
        in_specs.append(_const_spec(arr.shape))
        args.append(arr)

    if chained:
        st_shape = lambda n: jax.ShapeDtypeStruct((n_seq, n, D_MODEL), F32)
        st_spec = lambda n: pl.BlockSpec((None, n, D_MODEL), lambda b, j: (b, 0, 0))
    else:
        st_shape = lambda n: jax.ShapeDtypeStruct((n_seq, n, SUBLANES, D_MODEL), F32)
        st_spec = lambda n: pl.BlockSpec((None, n, SUBLANES, D_MODEL), lambda b, j: (b, 0, 0, 0))
    halos = (K_A - 1, K_B - 1, POOL_PAD)
    out_shape = [jax.ShapeDtypeStruct((n_tiles * T, D_MODEL), F32)] + [st_shape(n) for n in halos]
    out_specs = [pl.BlockSpec((T, D_MODEL), tile_idx)] + [st_spec(n) for n in halos]

    scratch = [
        pltpu.VMEM((T, D_MODEL), BF16),
        pltpu.VMEM((T, 3 * D_MODEL), F32),
        pltpu.VMEM((T, D_MODEL), F32),
        pltpu.VMEM((T, D_MODEL), F32),
    ] + [pltpu.VMEM((SUBLANES * (n + Q), D_MODEL), F32) for n in halos] + [
        pltpu.VMEM((T, D_MODEL), BF16),
        pltpu.VMEM((T, D_MODEL), BF16),
        pltpu.VMEM((T, D_MODEL), BF16),
    ]
    if chained:
        scratch += [pltpu.VMEM((SUBLANES * n, D_MODEL), F32) for n in halos]
    if permute_in:
        scratch.append(pltpu.VMEM((T, D_MODEL), F32))

    return pl.pallas_call(
        functools.partial(_mixer_kernel, q_frames=Q, sps=sps, chained=chained,
                          permute_in=permute_in, pos0=pos0),
        grid=(n_seq, tiles_per_seq),
        in_specs=in_specs,
        out_specs=out_specs,
        out_shape=out_shape,
        scratch_shapes=scratch,
        compiler_params=pltpu.CompilerParams(
            dimension_semantics=("arbitrary", "arbitrary"),
            vmem_limit_bytes=VMEM_LIMIT_BYTES),
        name="mixer_chained" if chained else "mixer_streams",
    )(*args)


def _ffn_kernel(x_ref, ada_ref, n2g_ref, wfi_ref, wfo_ref, fg_ref, o_ref,
                h_ref, g_ref, act_ref, y_ref, *, q_frames, final):
    Q = q_frames
    T = SUBLANES * Q
    n_blocks = T // ROWS

    n2g = n2g_ref[...]
    def norm_body(i, carry):
        x = x_ref[_rows(i), :]
        h = _rms(x) * n2g * (1.0 + ada_ref[A_S2]) + ada_ref[A_SH2]
        h_ref[_rows(i), :] = h.astype(BF16)
        return carry
    lax.fori_loop(0, n_blocks, norm_body, 0)

    g_ref[...] = _dot(h_ref[...], wfi_ref[...])
    def act_body(i, carry):
        act_ref[_rows(i), :] = (_silu(g_ref[_rows(i), 0:D_FF])
                                * g_ref[_rows(i), D_FF:2 * D_FF]).astype(BF16)
        return carry
    lax.fori_loop(0, n_blocks, act_body, 0)

    y_ref[...] = _dot(act_ref[...], wfo_ref[...])
    if final:
        fg = fg_ref[...]
        def out_body(q, carry):
            rows = _rows(q, SUBLANES)
            xn = x_ref[rows, :] + ada_ref[A_G2, 0:SUBLANES, :] * y_ref[rows, :]
            y = _rms(xn) * fg
            for c in range(N_CHUNKS):
                o_ref[pl.ds(q * N_CHUNKS + c, SUBLANES, stride=N_CHUNKS * Q), :] = (
                    y[:, c * LANES:(c + 1) * LANES])
            return carry
        lax.fori_loop(0, Q, out_body, 0)
    else:
        def out_body(i, carry):
            o_ref[_rows(i), :] = x_ref[_rows(i), :] + ada_ref[A_G2] * y_ref[_rows(i), :]
            return carry
        lax.fori_loop(0, n_blocks, out_body, 0)


def _ffn_call(x, ada, n2g, wfi, wfo, fg, *, n_seq, tiles_per_seq, q_frames, final):
    Q = q_frames
    T = SUBLANES * Q
    n_tiles = n_seq * tiles_per_seq
    tile_idx = lambda b, j: (b * tiles_per_seq + j, 0)
    if final:
        out_shape = jax.ShapeDtypeStruct((n_tiles * T * N_CHUNKS, LANES), F32)
        out_spec = pl.BlockSpec((T * N_CHUNKS, LANES), tile_idx)
    else:
        out_shape = jax.ShapeDtypeStruct((n_tiles * T, D_MODEL), F32)
        out_spec = pl.BlockSpec((T, D_MODEL), tile_idx)
    return pl.pallas_call(
        functools.partial(_ffn_kernel, q_frames=Q, final=final),
        grid=(n_seq, tiles_per_seq),
        in_specs=[
            pl.BlockSpec((T, D_MODEL), tile_idx),
            pl.BlockSpec((None, 6, ROWS, D_MODEL), lambda b, j: (b, 0, 0, 0)),
            _const_spec(n2g.shape), _const_spec(wfi.shape), _const_spec(wfo.shape),
            _const_spec(fg.shape),
        ],
        out_specs=out_spec,
        out_shape=out_shape,
        scratch_shapes=[
            pltpu.VMEM((T, D_MODEL), BF16),
            pltpu.VMEM((T, 2 * D_FF), F32),
            pltpu.VMEM((T, D_FF), BF16),
            pltpu.VMEM((T, D_MODEL), F32),
        ],
        compiler_params=pltpu.CompilerParams(
            dimension_semantics=("arbitrary", "arbitrary"),
            vmem_limit_bytes=VMEM_LIMIT_BYTES),
        name="ffn_final" if final else "ffn",
    )(x, ada, n2g, wfi, wfo, fg)


def _run_group(x, ada, hists, lw, final_g, *, n_seq, tiles_per_seq, q_frames,
               sps, chained, pos0):
    states = []
    for l in range(DEPTH):
        w = lw[l]
        x, sa, sb, sp = _mixer_call(
            x, ada[l], None if chained else hists[l], w["mixer"],
            n_seq=n_seq, tiles_per_seq=tiles_per_seq, q_frames=q_frames, sps=sps,
            chained=chained, permute_in=(l == 0), pos0=pos0)
        x = _ffn_call(x, ada[l], *w["ffn"], final_g, n_seq=n_seq,
                      tiles_per_seq=tiles_per_seq, q_frames=q_frames,
                      final=(l == DEPTH - 1))
        states.append((sa, sb, sp))
    return x, states


def _layer_weights(l, norm1_g, w_in, w_conv_a, w_out_a, w_conv_b, b_conv_b, ln_b_g,
                   ln_b_b, w_out_b, w_pool, pool_scale, w_o, norm2_g, w_ffn_in,
                   w_ffn_out):
    row = lambda v: v[l].reshape(1, -1)
    mixer = [row(norm1_g), w_in[l].astype(BF16), w_conv_a[l], w_out_a[l].astype(BF16),
             w_conv_b[l], row(b_conv_b), row(ln_b_g), row(ln_b_b),
             w_out_b[l].astype(BF16), w_pool[l].astype(BF16), row(pool_scale),
             w_o[l].astype(BF16)]
    ffn = [row(norm2_g), w_ffn_in[l].astype(BF16), w_ffn_out[l].astype(BF16)]
    return {"mixer": mixer, "ffn": ffn}


def kernel(x_prompt, x_sample, c_prompt, c_sample, cache_conv_a, cache_conv_b, cache_pool, w_ada, b_ada, norm1_g, w_in, w_conv_a, w_out_a, w_conv_b, b_conv_b, ln_b_g, ln_b_b, w_out_b, w_pool, pool_scale, w_o, norm2_g, w_ffn_in, w_ffn_out, final_g):
    bp, sp_len, _ = x_prompt.shape
    bs, ss_len, _ = x_sample.shape
    q_frames = 32
    tile = SUBLANES * q_frames
    assert q_frames >= K_B - 1 and sp_len % tile == 0 and ss_len % q_frames == 0
    sps_s = ss_len // q_frames
    seq_per_tile = SUBLANES // sps_s
    assert SUBLANES % sps_s == 0 and bs % seq_per_tile == 0
    n_sgrp = bs // seq_per_tile

    lw = [_layer_weights(l, norm1_g, w_in, w_conv_a, w_out_a, w_conv_b, b_conv_b,
                         ln_b_g, ln_b_b, w_out_b, w_pool, pool_scale, w_o, norm2_g,
                         w_ffn_in, w_ffn_out) for l in range(DEPTH)]
    fg = final_g.reshape(1, D_MODEL)

    ada = _ada_call(jnp.concatenate([c_prompt, c_sample], axis=0), w_ada, b_ada)
    ada = ada.reshape(DEPTH, bp + bs, 6, D_MODEL)
    ada_p = jnp.broadcast_to(ada[:, :bp, :, None, :], (DEPTH, bp, 6, ROWS, D_MODEL))
    ada_s = ada[:, bp:].reshape(DEPTH, n_sgrp, seq_per_tile, 6, D_MODEL)
    ada_s = jnp.repeat(jnp.transpose(ada_s, (0, 1, 3, 2, 4)), sps_s, axis=3)
    ada_s = jnp.tile(ada_s, (1, 1, 1, ROWS // SUBLANES, 1))

    def frame_major(cache):
        n = cache.shape[2]
        c = cache.reshape(DEPTH, n_sgrp, seq_per_tile, n, D_MODEL)
        return jnp.repeat(jnp.transpose(c, (0, 1, 3, 2, 4)), sps_s, axis=3)
    hists = list(zip(frame_major(cache_conv_a), frame_major(cache_conv_b),
                     frame_major(cache_pool)))

    yp, st_p = _run_group(
        x_prompt.reshape(bp * sp_len * N_CHUNKS, LANES), ada_p, None, lw, fg,
        n_seq=bp, tiles_per_seq=sp_len // tile, q_frames=q_frames,
        sps=SUBLANES, chained=True, pos0=0)
    ys, st_s = _run_group(
        x_sample.reshape(bs * ss_len * N_CHUNKS, LANES), ada_s, hists, lw, fg,
        n_seq=n_sgrp, tiles_per_seq=1, q_frames=q_frames, sps=sps_s, chained=False,
        pos0=PAST_LEN)

    def seq_major(s):
        n = s.shape[1]
        s = s[:, :, sps_s - 1::sps_s, :]
        return jnp.transpose(s, (0, 2, 1, 3)).reshape(bs, n, D_MODEL)

    return (yp.reshape(bp, sp_len, D_MODEL), ys.reshape(bs, ss_len, D_MODEL),
            jnp.stack([s[0] for s in st_p]), jnp.stack([s[1] for s in st_p]),
            jnp.stack([s[2] for s in st_p]),
            jnp.stack([seq_major(s[0]) for s in st_s]),
            jnp.stack([seq_major(s[1]) for s in st_s]),
            jnp.stack([seq_major(s[2]) for s in st_s]))
```

```python
import functools

import jax
import jax.numpy as jnp
from jax import lax
from jax.experimental import pallas as pl
from jax.experimental.pallas import tpu as pltpu

F32 = jnp.float32
BF16 = jnp.bfloat16

D_MODEL = 1024
DEPTH = 2
EPS = 1e-6
K_A = 3
K_B = 31
POOL_WINDOWS = (2, 4, 8, 16)
N_POOL = len(POOL_WINDOWS)
POOL_GROUP = D_MODEL // N_POOL
POOL_PAD = max(POOL_WINDOWS) - 1
N_IN = 9 * D_MODEL
D_FF = 2816
PAST_LEN = 1024

SUBLANES = 8
LANES = 128
ROWS = 16
N_CHUNKS = D_MODEL // LANES
VMEM_LIMIT_BYTES = 60 * 1024 * 1024

C_BG, C_CG, C_HA, C_GA, C_GB, C_PIN, C_G0, C_G1, C_G2 = (i * D_MODEL for i in range(9))
C_YA, C_YB, C_YC, C_M, C_OUT = C_BG, C_CG, C_HA, C_GA, C_GB

A_SH1, A_S1, A_G1, A_SH2, A_S2, A_G2 = range(6)


def _sigmoid(x):
    return 0.5 * jnp.tanh(0.5 * x) + 0.5


def _silu(x):
    hx = 0.5 * x
    return hx + hx * jnp.tanh(hx)


def _dot(a, b):
    return jnp.dot(a, b, preferred_element_type=F32)


def _blk(i, n=ROWS):
    return pl.ds(pl.multiple_of(i * n, n), n)


def _at(start, n, align):
    return pl.ds(pl.multiple_of(start, align), n)


def _cols(c0):
    return slice(c0, c0 + D_MODEL)


def _loop(n, body, unroll=1):
    def step(i, carry):
        body(i)
        return carry
    lax.fori_loop(0, n, step, 0, unroll=unroll)


def _rms(x):
    return x * lax.rsqrt(jnp.mean(x * x, axis=-1, keepdims=True) + EPS)


def _ada_kernel(c_ref, w_ref, b_ref, o_ref):
    sc = _silu(c_ref[...]).astype(BF16)
    o_ref[...] = _dot(sc, w_ref[...].astype(BF16)) + b_ref[...]


def _ada_call(c_all, w_ada, b_ada):
    n = c_all.shape[0]
    tn = 1536
    return pl.pallas_call(
        _ada_kernel,
        grid=(DEPTH, 6 * D_MODEL // tn),
        in_specs=[
            pl.BlockSpec((n, D_MODEL), lambda l, j: (0, 0)),
            pl.BlockSpec((None, D_MODEL, tn), lambda l, j: (l, 0, j)),
            pl.BlockSpec((None, 1, tn), lambda l, j: (l, 0, j)),
        ],
        out_specs=pl.BlockSpec((None, n, tn), lambda l, j: (l, 0, j)),
        out_shape=jax.ShapeDtypeStruct((DEPTH, n, 6 * D_MODEL), F32),
        compiler_params=pltpu.CompilerParams(
            dimension_semantics=("arbitrary", "arbitrary"),
            vmem_limit_bytes=VMEM_LIMIT_BYTES),
        name="ada",
    )(c_all, w_ada, b_ada.reshape(DEPTH, 1, 6 * D_MODEL))


def _mixer_kernel(*refs, q_frames, sps, chained, permute_in, pos0):
    Q = q_frames
    T = SUBLANES * Q
    refs = list(refs)
    n_x = N_CHUNKS if permute_in else 1
    x_refs = refs[:n_x]
    ada_ref = refs[n_x]
    refs = refs[n_x + 1:]
    if not chained:
        hist_a_ref, hist_b_ref, hist_p_ref = refs[:3]
        refs = refs[3:]
    else:
        hist_a_ref = hist_b_ref = hist_p_ref = None
    (n1g_ref, win_ref, wca_ref, woa_ref, wcb_ref, bcb_ref, lng_ref, lnb_ref,
     wob_ref, wpool_ref, pscale_ref, wo_ref) = refs[:12]
    refs = refs[12:]
    xo_ref, sa_ref, sb_ref, sp_ref = refs[:4]
    refs = refs[4:]
    h_ref, z_ref, cb_ref, eu_ref, ev_ref, ep_ref, a_ref, b_ref, c_ref = refs[:9]
    refs = refs[9:]
    if chained:
        cu_ref, cv_ref, cp_ref = refs[:3]
        refs = refs[3:]
    else:
        cu_ref = cv_ref = cp_ref = None
    xs_ref = refs[0] if permute_in else x_refs[0]

    j = pl.program_id(1)
    n_blocks = T // ROWS

    if permute_in:
        def perm_body(q):
            for c in range(N_CHUNKS):
                xs_ref[_blk(q, SUBLANES), c * LANES:(c + 1) * LANES] = (
                    x_refs[c][pl.ds(q, SUBLANES, stride=Q), :])
        _loop(Q, perm_body, unroll=2)

    if chained:
        @pl.when(j == 0)
        def _():
            cu_ref[...] = jnp.zeros_like(cu_ref)
            cv_ref[...] = jnp.zeros_like(cv_ref)
            cp_ref[...] = jnp.zeros_like(cp_ref)

    def fill_halo(e_ref, carry_ref, hist_ref, n_halo):
        stream = lax.broadcasted_iota(jnp.int32, (SUBLANES, D_MODEL), 0)
        first = (stream & (sps - 1)) == 0
        for g in range(n_halo):
            dst = slice(SUBLANES * g, SUBLANES * (g + 1))
            tail = e_ref[SUBLANES * (Q + g):SUBLANES * (Q + g + 1), :]
            prev = pltpu.roll(tail, 1, axis=0)
            hist = carry_ref[dst, :] if chained else hist_ref[g]
            e_ref[dst, :] = jnp.where(first, hist, prev)
            if chained:
                carry_ref[dst, :] = prev

    n1g = n1g_ref[...]
    def norm_body(i):
        x = xs_ref[_blk(i), :]
        h = _rms(x) * n1g * (1.0 + ada_ref[A_S1]) + ada_ref[A_SH1]
        h_ref[_blk(i), :] = h.astype(BF16)
    _loop(n_blocks, norm_body, unroll=4)

    z_ref[...] = _dot(h_ref[...], win_ref[...])

    def ext_body(i):
        u = z_ref[_blk(i), _cols(C_CG)] * z_ref[_blk(i), _cols(C_HA)]
        eu_ref[_at(i * ROWS + SUBLANES * (K_A - 1), ROWS, SUBLANES), :] = u
        v = z_ref[_blk(i), _cols(C_GA)] * _sigmoid(z_ref[_blk(i), _cols(C_GB)])
        ev_ref[_at(i * ROWS + SUBLANES * (K_B - 1), ROWS, SUBLANES), :] = v
        ep_ref[_at(i * ROWS + SUBLANES * POOL_PAD, ROWS, SUBLANES), :] = (
            z_ref[_blk(i), _cols(C_PIN)])
    _loop(n_blocks, ext_body, unroll=2)
    fill_halo(eu_ref, cu_ref, hist_a_ref, K_A - 1)
    fill_halo(ev_ref, cv_ref, hist_b_ref, K_B - 1)
    fill_halo(ep_ref, cp_ref, hist_p_ref, POOL_PAD)

    def conv_a_body(i):
        acc = None
        for k in range(K_A):
            win = eu_ref[_at(i * ROWS + SUBLANES * k, ROWS, SUBLANES), :]
            t = win * wca_ref[k:k + 1, :]
            acc = t if acc is None else acc + t
        a_ref[_blk(i), :] = (z_ref[_blk(i), _cols(C_BG)] * acc).astype(BF16)
    _loop(n_blocks, conv_a_body, unroll=2)

    conv_rows = 64
    def conv_b_body(c):
        col = _blk(c, LANES)
        for rb in range(T // conv_rows):
            r0 = rb * conv_rows
            acc = jnp.broadcast_to(bcb_ref[:, col], (conv_rows, LANES))
            for k in range(K_B):
                win = ev_ref[r0 + SUBLANES * k:r0 + SUBLANES * k + conv_rows, col]
                acc = acc + win * wcb_ref[k:k + 1, col]
            cb_ref[r0:r0 + conv_rows, col] = acc
    _loop(N_CHUNKS, conv_b_body)

    lng = lng_ref[...]
    lnb = lnb_ref[...]
    def ln_body(i):
        cb = cb_ref[_blk(i), :]
        mu = jnp.mean(cb, axis=-1, keepdims=True)
        d = cb - mu
        var = jnp.mean(d * d, axis=-1, keepdims=True)
        y = d * lax.rsqrt(var + EPS) * lng + lnb
        b_ref[_blk(i), :] = _silu(y).astype(BF16)
    _loop(n_blocks, ln_body, unroll=4)

    def pool_body(i):
        base = i * ROWS + SUBLANES * POOL_PAD
        if pos0 < POOL_PAD:
            rr = lax.broadcasted_iota(jnp.int32, (ROWS, 1), 0)
            frame = j * T + (rr & (SUBLANES - 1)) * Q + (i * (ROWS // SUBLANES) + (rr >> 3))
            avail = pos0 + frame + 1
        outs = []
        for g, w in enumerate(POOL_WINDOWS):
            cols = slice(g * POOL_GROUP, (g + 1) * POOL_GROUP)
            cur = ep_ref[_at(base, ROWS, SUBLANES), cols]
            acc = cur
            for s in range(1, w):
                acc = acc + ep_ref[_at(base - SUBLANES * s, ROWS, SUBLANES), cols]
            if pos0 < POOL_PAD:
                inv = 1.0 / jnp.minimum(avail, w).astype(F32)
            else:
                inv = 1.0 / w
            outs.append(acc * inv - cur)
        c_ref[_blk(i), :] = jnp.concatenate(outs, axis=-1).astype(BF16)
    _loop(n_blocks, pool_body, unroll=2)

    z_ref[:, _cols(C_YA)] = _dot(a_ref[...], woa_ref[...])
    z_ref[:, _cols(C_YB)] = _dot(b_ref[...], wob_ref[...])
    for g in range(N_POOL):
        cols = slice(g * POOL_GROUP, (g + 1) * POOL_GROUP)
        z_ref[:, C_YC + g * POOL_GROUP:C_YC + (g + 1) * POOL_GROUP] = _dot(
            c_ref[:, cols], wpool_ref[g])

    pscale = pscale_ref[...]
    def merge_body(i):
        m = (_sigmoid(z_ref[_blk(i), _cols(C_G0)]) * z_ref[_blk(i), _cols(C_YA)]
             + _sigmoid(z_ref[_blk(i), _cols(C_G1)]) * z_ref[_blk(i), _cols(C_YB)]
             + _sigmoid(z_ref[_blk(i), _cols(C_G2)]) * (z_ref[_blk(i), _cols(C_YC)] * pscale))
        h_ref[_blk(i), :] = m.astype(BF16)
    _loop(n_blocks, merge_body, unroll=2)

    z_ref[:, _cols(C_OUT)] = _dot(h_ref[...], wo_ref[...])
    def res_body(i):
        xo_ref[_blk(i), :] = (xs_ref[_blk(i), :]
                              + ada_ref[A_G1] * z_ref[_blk(i), _cols(C_OUT)])
    _loop(n_blocks, res_body, unroll=2)

    if chained:
        @pl.when(j == pl.num_programs(1) - 1)
        def _():
            for carry_ref, s_ref, n_halo in ((cu_ref, sa_ref, K_A - 1),
                                             (cv_ref, sb_ref, K_B - 1),
                                             (cp_ref, sp_ref, POOL_PAD)):
                for g in range(n_halo):
                    s_ref[g:g + 1, :] = carry_ref[SUBLANES * g:SUBLANES * g + 1, :]
    else:
        for e_ref, s_ref, n_halo in ((eu_ref, sa_ref, K_A - 1),
                                     (ev_ref, sb_ref, K_B - 1),
                                     (ep_ref, sp_ref, POOL_PAD)):
            for g in range(n_halo):
                s_ref[g] = e_ref[SUBLANES * (Q + g):SUBLANES * (Q + g + 1), :]


def _const_spec(shape):
    nd = len(shape)
    return pl.BlockSpec(shape, lambda *_: (0,) * nd, pipeline_mode=pl.Buffered(1))


def _mixer_call(x, ada, hists, w, *, n_seq, tiles_per_seq, q_frames, sps, chained,
                permute_in, pos0):
    Q = q_frames
    T = SUBLANES * Q
    n_tiles = n_seq * tiles_per_seq
    tile_idx = lambda b, j: (b * tiles_per_seq + j, 0)
    if permute_in:
        in_specs = [pl.BlockSpec((T, LANES), lambda b, j, c=c: (b * tiles_per_seq + j, c))
                    for c in range(N_CHUNKS)]
        args = [x] * N_CHUNKS
    else:
        in_specs = [pl.BlockSpec((T, D_MODEL), tile_idx)]
        args = [x]
    in_specs.append(pl.BlockSpec((None, 6, ROWS, D_MODEL), lambda b, j: (b, 0, 0, 0)))
    args.append(ada)
    if not chained:
        for hst in hists:
            in_specs.append(pl.BlockSpec((None,) + hst.shape[1:], lambda b, j: (b, 0, 0, 0)))
            args.append(hst)
    for arr in w:
        in_specs.append(_const_spec(arr.shape))
        args.append(arr)

    if chained:
        st_shape = lambda n: jax.ShapeDtypeStruct((n_seq, n, D_MODEL), F32)
        st_spec = lambda n: pl.BlockSpec((None, n, D_MODEL), lambda b, j: (b, 0, 0))
    else:
        st_shape = lambda n: jax.ShapeDtypeStruct((n_seq, n, SUBLANES, D_MODEL), F32)
        st_spec = lambda n: pl.BlockSpec((None, n, SUBLANES, D_MODEL), lambda b, j: (b, 0, 0, 0))
    halos = (K_A - 1, K_B - 1, POOL_PAD)
    out_shape = [jax.ShapeDtypeStruct((n_tiles * T, D_MODEL), F32)] + [st_shape(n) for n in halos]
    out_specs = [pl.BlockSpec((T, D_MODEL), tile_idx)] + [st_spec(n) for n in halos]

    scratch = [
        pltpu.VMEM((T, D_MODEL), BF16),
        pltpu.VMEM((T, N_IN), F32),
        pltpu.VMEM((T, D_MODEL), F32),
    ] + [pltpu.VMEM((SUBLANES * (n + Q), D_MODEL), F32) for n in halos] + [
        pltpu.VMEM((T, D_MODEL), BF16),
        pltpu.VMEM((T, D_MODEL), BF16),
        pltpu.VMEM((T, D_MODEL), BF16),
    ]
    if chained:
        scratch += [pltpu.VMEM((SUBLANES * n, D_MODEL), F32) for n in halos]
    if permute_in:
        scratch.append(pltpu.VMEM((T, D_MODEL), F32))

    return pl.pallas_call(
        functools.partial(_mixer_kernel, q_frames=Q, sps=sps, chained=chained,
                          permute_in=permute_in, pos0=pos0),
        grid=(n_seq, tiles_per_seq),
        in_specs=in_specs,
        out_specs=out_specs,
        out_shape=out_shape,
        scratch_shapes=scratch,
        compiler_params=pltpu.CompilerParams(
            dimension_semantics=("arbitrary", "arbitrary"),
            vmem_limit_bytes=VMEM_LIMIT_BYTES),
        name="mixer_chained" if chained else "mixer_streams",
    )(*args)


def _ffn_kernel(*refs, q_frames, final):
    Q = q_frames
    T = SUBLANES * Q
    n_blocks = T // ROWS
    x_ref, ada_ref, n2g_ref, wfi_ref, wfo_ref, fg_ref, o_ref = refs[:7]
    h_ref, g_ref, act_ref, y_ref = refs[7:11]

    n2g = n2g_ref[...]
    def norm_body(i):
        x = x_ref[_blk(i), :]
        h = _rms(x) * n2g * (1.0 + ada_ref[A_S2]) + ada_ref[A_SH2]
        h_ref[_blk(i), :] = h.astype(BF16)
    _loop(n_blocks, norm_body, unroll=4)

    g_ref[...] = _dot(h_ref[...], wfi_ref[...])
    def act_body(i):
        act_ref[_blk(i), :] = (_silu(g_ref[_blk(i), 0:D_FF])
                               * g_ref[_blk(i), D_FF:2 * D_FF]).astype(BF16)
    _loop(n_blocks, act_body, unroll=2)

    y_ref[...] = _dot(act_ref[...], wfo_ref[...])
    if final:
        ys_ref = refs[11]
        fg = fg_ref[...]
        def out_body(i):
            xn = x_ref[_blk(i), :] + ada_ref[A_G2] * y_ref[_blk(i), :]
            y = _rms(xn) * fg
            for c in range(N_CHUNKS):
                ys_ref[_at(c * T + i * ROWS, ROWS, ROWS), :] = y[:, c * LANES:(c + 1) * LANES]
        _loop(n_blocks, out_body, unroll=4)
        def unperm_body(i):
            r = i & (SUBLANES - 1)
            q0 = (i >> 3) * SUBLANES
            for c in range(N_CHUNKS):
                o_ref[_at(r * Q + q0, SUBLANES, SUBLANES), c * LANES:(c + 1) * LANES] = (
                    ys_ref[pl.ds(c * T + SUBLANES * q0 + r, SUBLANES, stride=SUBLANES), :])
        _loop(T // SUBLANES, unperm_body, unroll=2)
    else:
        def out_body(i):
            o_ref[_blk(i), :] = x_ref[_blk(i), :] + ada_ref[A_G2] * y_ref[_blk(i), :]
        _loop(n_blocks, out_body, unroll=2)


def _ffn_call(x, ada, n2g, wfi, wfo, fg, *, n_seq, tiles_per_seq, q_frames, final):
    Q = q_frames
    T = SUBLANES * Q
    n_tiles = n_seq * tiles_per_seq
    tile_idx = lambda b, j: (b * tiles_per_seq + j, 0)
    scratch = [
        pltpu.VMEM((T, D_MODEL), BF16),
        pltpu.VMEM((T, 2 * D_FF), F32),
        pltpu.VMEM((T, D_FF), BF16),
        pltpu.VMEM((T, D_MODEL), F32),
    ]
    if final:
        scratch.append(pltpu.VMEM((N_CHUNKS * T, LANES), F32))
    return pl.pallas_call(
        functools.partial(_ffn_kernel, q_frames=Q, final=final),
        grid=(n_seq, tiles_per_seq),
        in_specs=[
            pl.BlockSpec((T, D_MODEL), tile_idx),
            pl.BlockSpec((None, 6, ROWS, D_MODEL), lambda b, j: (b, 0, 0, 0)),
            _const_spec(n2g.shape), _const_spec(wfi.shape), _const_spec(wfo.shape),
            _const_spec(fg.shape),
        ],
        out_specs=pl.BlockSpec((T, D_MODEL), tile_idx),
        out_shape=jax.ShapeDtypeStruct((n_tiles * T, D_MODEL), F32),
        scratch_shapes=scratch,
        compiler_params=pltpu.CompilerParams(
            dimension_semantics=("arbitrary", "arbitrary"),
            vmem_limit_bytes=VMEM_LIMIT_BYTES),
        name="ffn_final" if final else "ffn",
    )(x, ada, n2g, wfi, wfo, fg)


def _run_group(x, ada, hists, lw, final_g, *, n_seq, tiles_per_seq, q_frames,
               sps, chained, pos0):
    states = []
    for l in range(DEPTH):
        w = lw[l]
        x, sa, sb, sp = _mixer_call(
            x, ada[l], None if chained else hists[l], w["mixer"],
            n_seq=n_seq, tiles_per_seq=tiles_per_seq, q_frames=q_frames, sps=sps,
            chained=chained, permute_in=(l == 0), pos0=pos0)
        x = _ffn_call(x, ada[l], *w["ffn"], final_g, n_seq=n_seq,
                      tiles_per_seq=tiles_per_seq, q_frames=q_frames,
                      final=(l == DEPTH - 1))
        states.append((sa, sb, sp))
    return x, states


def _layer_weights(l, norm1_g, w_in, w_conv_a, w_out_a, w_conv_b, b_conv_b, ln_b_g,
                   ln_b_b, w_out_b, w_pool, pool_scale, w_o, norm2_g, w_ffn_in,
                   w_ffn_out):
    row = lambda v: v[l].reshape(1, -1)
    mixer = [row(norm1_g), w_in[l].astype(BF16), w_conv_a[l], w_out_a[l].astype(BF16),
             w_conv_b[l], row(b_conv_b), row(ln_b_g), row(ln_b_b),
             w_out_b[l].astype(BF16), w_pool[l].astype(BF16), row(pool_scale),
             w_o[l].astype(BF16)]
    ffn = [row(norm2_g), w_ffn_in[l].astype(BF16), w_ffn_out[l].astype(BF16)]
    return {"mixer": mixer, "ffn": ffn}


def kernel(x_prompt, x_sample, c_prompt, c_sample, cache_conv_a, cache_conv_b, cache_pool, w_ada, b_ada, norm1_g, w_in, w_conv_a, w_out_a, w_conv_b, b_conv_b, ln_b_g, ln_b_b, w_out_b, w_pool, pool_scale, w_o, norm2_g, w_ffn_in, w_ffn_out, final_g):
    bp, sp_len, _ = x_prompt.shape
    bs, ss_len, _ = x_sample.shape
    q_frames = 32
    tile = SUBLANES * q_frames
    assert q_frames >= K_B - 1 and sp_len % tile == 0 and ss_len % q_frames == 0
    sps_s = ss_len // q_frames
    seq_per_tile = SUBLANES // sps_s
    assert SUBLANES % sps_s == 0 and bs % seq_per_tile == 0
    n_sgrp = bs // seq_per_tile

    lw = [_layer_weights(l, norm1_g, w_in, w_conv_a, w_out_a, w_conv_b, b_conv_b,
                         ln_b_g, ln_b_b, w_out_b, w_pool, pool_scale, w_o, norm2_g,
                         w_ffn_in, w_ffn_out) for l in range(DEPTH)]
    fg = final_g.reshape(1, D_MODEL)

    ada = _ada_call(jnp.concatenate([c_prompt, c_sample], axis=0), w_ada, b_ada)
    ada = ada.reshape(DEPTH, bp + bs, 6, D_MODEL)
    ada_p = jnp.broadcast_to(ada[:, :bp, :, None, :], (DEPTH, bp, 6, ROWS, D_MODEL))
    ada_s = ada[:, bp:].reshape(DEPTH, n_sgrp, seq_per_tile, 6, D_MODEL)
    ada_s = jnp.repeat(jnp.transpose(ada_s, (0, 1, 3, 2, 4)), sps_s, axis=3)
    ada_s = jnp.tile(ada_s, (1, 1, 1, ROWS // SUBLANES, 1))

    def frame_major(cache):
        n = cache.shape[2]
        c = cache.reshape(DEPTH, n_sgrp, seq_per_tile, n, D_MODEL)
        return jnp.repeat(jnp.transpose(c, (0, 1, 3, 2, 4)), sps_s, axis=3)
    hists = list(zip(frame_major(cache_conv_a), frame_major(cache_conv_b),
                     frame_major(cache_pool)))

    yp, st_p = _run_group(
        x_prompt.reshape(bp * sp_len, D_MODEL), ada_p, None, lw, fg,
        n_seq=bp, tiles_per_seq=sp_len // tile, q_frames=q_frames,
        sps=SUBLANES, chained=True, pos0=0)
    ys, st_s = _run_group(
        x_sample.reshape(bs * ss_len, D_MODEL), ada_s, hists, lw, fg,
        n_seq=n_sgrp, tiles_per_seq=1, q_frames=q_frames, sps=sps_s, chained=False,
        pos0=PAST_LEN)

    def seq_major(s):
        n = s.shape[1]
        s = s[:, :, sps_s - 1::sps_s, :]
        return jnp.transpose(s, (0, 2, 1, 3)).reshape(bs, n, D_MODEL)

    return (yp.reshape(bp, sp_len, D_MODEL), ys.reshape(bs, ss_len, D_MODEL),
            jnp.stack([s[0] for s in st_p]), jnp.stack([s[1] for s in st_p]),
            jnp.stack([s[2] for s in st_p]),
            jnp.stack([seq_major(s[0]) for s in st_s]),
            jnp.stack([seq_major(s[1]) for s in st_s]),
            jnp.stack([seq_major(s[2]) for s in st_s]))
```

```python
import functools

import jax
import jax.numpy as jnp
from jax import lax
from jax.experimental import pallas as pl
from jax.experimental.pallas import tpu as pltpu

F32 = jnp.float32
BF16 = jnp.bfloat16

D_MODEL = 1024
DEPTH = 2
EPS = 1e-6
K_A = 3
K_B = 31
POOL_WINDOWS = (2, 4, 8, 16)
N_POOL = len(POOL_WINDOWS)
POOL_GROUP = D_MODEL // N_POOL
POOL_PAD = max(POOL_WINDOWS) - 1
N_IN = 9 * D_MODEL
D_FF = 2816
PAST_LEN = 1024

SUBLANES = 8
LANES = 128
ROWS = 16
N_CHUNKS = D_MODEL // LANES
MXU_COLS = 256
N_COLBLK = D_MODEL // MXU_COLS
VMEM_LIMIT_BYTES = 60 * 1024 * 1024

C_BG, C_CG, C_HA, C_GA, C_GB, C_PIN, C_G0, C_G1, C_G2 = (i * D_MODEL for i in range(9))

A_SH1, A_S1, A_G1, A_SH2, A_S2, A_G2 = range(6)


def _sigmoid(x):
    return 0.5 * jnp.tanh(0.5 * x) + 0.5


def _silu(x):
    hx = 0.5 * x
    return hx + hx * jnp.tanh(hx)


def _dot(a, b):
    return jnp.dot(a, b, preferred_element_type=F32)


def _blk(i, n=ROWS):
    return pl.ds(pl.multiple_of(i * n, n), n)


def _at(start, n, align):
    return pl.ds(pl.multiple_of(start, align), n)


def _colblk(n, c0=0):
    return slice(c0 + n * MXU_COLS, c0 + (n + 1) * MXU_COLS)


def _loop(n, body, unroll=1):
    def step(i, carry):
        body(i)
        return carry
    lax.fori_loop(0, n, step, 0, unroll=unroll)


def _rms(x):
    return x * lax.rsqrt(jnp.mean(x * x, axis=-1, keepdims=True) + EPS)


def _ada_kernel(c_ref, w_ref, b_ref, o_ref):
    sc = _silu(c_ref[...]).astype(BF16)
    o_ref[...] = _dot(sc, w_ref[...].astype(BF16)) + b_ref[...]


def _ada_call(c_all, w_ada, b_ada):
    n = c_all.shape[0]
    tn = 1536
    return pl.pallas_call(
        _ada_kernel,
        grid=(DEPTH, 6 * D_MODEL // tn),
        in_specs=[
            pl.BlockSpec((n, D_MODEL), lambda l, j: (0, 0)),
            pl.BlockSpec((None, D_MODEL, tn), lambda l, j: (l, 0, j)),
            pl.BlockSpec((None, 1, tn), lambda l, j: (l, 0, j)),
        ],
        out_specs=pl.BlockSpec((None, n, tn), lambda l, j: (l, 0, j)),
        out_shape=jax.ShapeDtypeStruct((DEPTH, n, 6 * D_MODEL), F32),
        compiler_params=pltpu.CompilerParams(
            dimension_semantics=("arbitrary", "arbitrary"),
            vmem_limit_bytes=VMEM_LIMIT_BYTES),
        name="ada",
    )(c_all, w_ada, b_ada.reshape(DEPTH, 1, 6 * D_MODEL))


def _mixer_kernel(*refs, q_frames, sps, chained, permute_in, pos0):
    Q = q_frames
    T = SUBLANES * Q
    refs = list(refs)
    n_x = N_CHUNKS if permute_in else 1
    x_refs = refs[:n_x]
    ada_ref = refs[n_x]
    refs = refs[n_x + 1:]
    if not chained:
        hist_a_ref, hist_b_ref, hist_p_ref = refs[:3]
        refs = refs[3:]
    else:
        hist_a_ref = hist_b_ref = hist_p_ref = None
    (n1g_ref, win_ref, wca_ref, woa_ref, wcb_ref, bcb_ref, lng_ref, lnb_ref,
     wob_ref, wpool_ref, pscale_ref, wo_ref) = refs[:12]
    refs = refs[12:]
    xo_ref, sa_ref, sb_ref, sp_ref = refs[:4]
    refs = refs[4:]
    (h_ref, sg_ref, m_ref, cb_ref, eu_ref, ev_ref, ep_ref,
     a_ref, b_ref, c_ref) = refs[:10]
    refs = refs[10:]
    if chained:
        cu_ref, cv_ref, cp_ref = refs[:3]
        refs = refs[3:]
    else:
        cu_ref = cv_ref = cp_ref = None
    xs_ref = refs[0] if permute_in else x_refs[0]

    j = pl.program_id(1)
    n_blocks = T // ROWS

    if permute_in:
        def perm_body(q):
            for c in range(N_CHUNKS):
                xs_ref[_blk(q, SUBLANES), c * LANES:(c + 1) * LANES] = (
                    x_refs[c][pl.ds(q, SUBLANES, stride=Q), :])
        _loop(Q, perm_body, unroll=2)

    if chained:
        @pl.when(j == 0)
        def _():
            cu_ref[...] = jnp.zeros_like(cu_ref)
            cv_ref[...] = jnp.zeros_like(cv_ref)
            cp_ref[...] = jnp.zeros_like(cp_ref)

    stream = lax.broadcasted_iota(jnp.int32, (SUBLANES, MXU_COLS), 0)
    first = (stream & (sps - 1)) == 0

    def fill_halo(e_ref, carry_ref, hist_ref, n_halo, cs):
        for g in range(n_halo):
            dst = slice(SUBLANES * g, SUBLANES * (g + 1))
            tail = e_ref[SUBLANES * (Q + g):SUBLANES * (Q + g + 1), cs]
            prev = pltpu.roll(tail, 1, axis=0)
            hist = carry_ref[dst, cs] if chained else hist_ref[g, :, cs]
            e_ref[dst, cs] = jnp.where(first, hist, prev)
            if chained:
                carry_ref[dst, cs] = prev

    n1g = n1g_ref[...]
    for i in range(n_blocks):
        rows = slice(i * ROWS, (i + 1) * ROWS)
        hrow = _rms(xs_ref[rows, :]) * n1g * (1.0 + ada_ref[A_S1]) + ada_ref[A_SH1]
        h_ref[rows, :] = hrow.astype(BF16)

    if pos0 < POOL_PAD:
        rr = lax.broadcasted_iota(jnp.int32, (T, 1), 0)
        frame = j * T + (rr & (SUBLANES - 1)) * Q + (rr >> 3)
        avail = pos0 + frame + 1

    h = h_ref[...]
    pscale = pscale_ref[...]
    for n in range(N_COLBLK):
        cs = _colblk(n)
        proj = lambda c0: _dot(h, win_ref[:, _colblk(n, c0)])

        eu_ref[SUBLANES * (K_A - 1):, cs] = proj(C_CG) * proj(C_HA)
        fill_halo(eu_ref, cu_ref, hist_a_ref, K_A - 1, cs)
        acc = None
        for k in range(K_A):
            t = eu_ref[SUBLANES * k:SUBLANES * k + T, cs] * wca_ref[k:k + 1, cs]
            acc = t if acc is None else acc + t
        a_ref[:, cs] = (proj(C_BG) * acc).astype(BF16)

        ev_ref[SUBLANES * (K_B - 1):, cs] = proj(C_GA) * _sigmoid(proj(C_GB))
        fill_halo(ev_ref, cv_ref, hist_b_ref, K_B - 1, cs)
        conv_rows = 64
        for cc in range(MXU_COLS // LANES):
            col = slice(cs.start + cc * LANES, cs.start + (cc + 1) * LANES)
            for rb in range(T // conv_rows):
                r0 = rb * conv_rows
                acc = jnp.broadcast_to(bcb_ref[:, col], (conv_rows, LANES))
                for k in range(K_B):
                    win = ev_ref[r0 + SUBLANES * k:r0 + SUBLANES * k + conv_rows, col]
                    acc = acc + win * wcb_ref[k:k + 1, col]
                cb_ref[r0:r0 + conv_rows, col] = acc

        for gi, c0 in enumerate((C_G0, C_G1, C_G2)):
            sg_ref[gi, :, cs] = _sigmoid(proj(c0))

        ep_ref[SUBLANES * POOL_PAD:, cs] = proj(C_PIN)
        fill_halo(ep_ref, cp_ref, hist_p_ref, POOL_PAD, cs)
        w = POOL_WINDOWS[n]
        cur = ep_ref[SUBLANES * POOL_PAD:, cs]
        acc = cur
        for s in range(1, w):
            acc = acc + ep_ref[SUBLANES * (POOL_PAD - s):SUBLANES * (POOL_PAD - s) + T, cs]
        if pos0 < POOL_PAD:
            inv = 1.0 / jnp.minimum(avail, w).astype(F32)
        else:
            inv = 1.0 / w
        pooled = (acc * inv - cur).astype(BF16)
        c_ref[:, cs] = pooled
        y_c = _dot(pooled, wpool_ref[n])
        m_ref[:, cs] = sg_ref[2, :, cs] * (y_c * pscale[:, cs])

    for n in range(N_COLBLK):
        cs = _colblk(n)
        m_ref[:, cs] += sg_ref[0, :, cs] * _dot(a_ref[...], woa_ref[:, cs])

    lng = lng_ref[...]
    lnb = lnb_ref[...]
    for i in range(n_blocks):
        rows = slice(i * ROWS, (i + 1) * ROWS)
        cb = cb_ref[rows, :]
        mu = jnp.mean(cb, axis=-1, keepdims=True)
        d = cb - mu
        var = jnp.mean(d * d, axis=-1, keepdims=True)
        y = d * lax.rsqrt(var + EPS) * lng + lnb
        b_ref[rows, :] = _silu(y).astype(BF16)

    for n in range(N_COLBLK):
        cs = _colblk(n)
        m = m_ref[:, cs] + sg_ref[1, :, cs] * _dot(b_ref[...], wob_ref[:, cs])
        h_ref[:, cs] = m.astype(BF16)

    for n in range(N_COLBLK):
        cs = _colblk(n)
        out = _dot(h_ref[...], wo_ref[:, cs])
        g1 = ada_ref[A_G1, :, cs]
        for i in range(n_blocks):
            rows = slice(i * ROWS, (i + 1) * ROWS)
            xo_ref[rows, cs] = xs_ref[rows, cs] + g1 * out[rows, :]

    if chained:
        @pl.when(j == pl.num_programs(1) - 1)
        def _():
            for carry_ref, s_ref, n_halo in ((cu_ref, sa_ref, K_A - 1),
                                             (cv_ref, sb_ref, K_B - 1),
                                             (cp_ref, sp_ref, POOL_PAD)):
                for g in range(n_halo):
                    s_ref[g:g + 1, :] = carry_ref[SUBLANES * g:SUBLANES * g + 1, :]
    else:
        for e_ref, s_ref, n_halo in ((eu_ref, sa_ref, K_A - 1),
                                     (ev_ref, sb_ref, K_B - 1),
                                     (ep_ref, sp_ref, POOL_PAD)):
            for g in range(n_halo):
                s_ref[g] = e_ref[SUBLANES * (Q + g):SUBLANES * (Q + g + 1), :]


def _layer_spec(arr, l):
    nd = arr.ndim - 1
    return pl.BlockSpec((None,) + arr.shape[1:], lambda *_: (l,) + (0,) * nd,
                        pipeline_mode=pl.Buffered(1))


def _mixer_call(x, ada, hists, w, l, *, n_seq, tiles_per_seq, q_frames, sps, chained,
                permute_in, pos0):
    Q = q_frames
    T = SUBLANES * Q
    n_tiles = n_seq * tiles_per_seq
    tile_idx = lambda b, j: (b * tiles_per_seq + j, 0)
    if permute_in:
        in_specs = [pl.BlockSpec((T, LANES), lambda b, j, c=c: (b * tiles_per_seq + j, c))
                    for c in range(N_CHUNKS)]
        args = [x] * N_CHUNKS
    else:
        in_specs = [pl.BlockSpec((T, D_MODEL), tile_idx)]
        args = [x]
    in_specs.append(pl.BlockSpec((None, 6, ROWS, D_MODEL), lambda b, j: (b, 0, 0, 0)))
    args.append(ada)
    if not chained:
        for hst in hists:
            in_specs.append(pl.BlockSpec((None,) + hst.shape[1:], lambda b, j: (b, 0, 0, 0)))
            args.append(hst)
    for arr in w:
        in_specs.append(_layer_spec(arr, l))
        args.append(arr)

    if chained:
        st_shape = lambda n: jax.ShapeDtypeStruct((n_seq, n, D_MODEL), F32)
        st_spec = lambda n: pl.BlockSpec((None, n, D_MODEL), lambda b, j: (b, 0, 0))
    else:
        st_shape = lambda n: jax.ShapeDtypeStruct((n_seq, n, SUBLANES, D_MODEL), F32)
        st_spec = lambda n: pl.BlockSpec((None, n, SUBLANES, D_MODEL), lambda b, j: (b, 0, 0, 0))
    halos = (K_A - 1, K_B - 1, POOL_PAD)
    out_shape = [jax.ShapeDtypeStruct((n_tiles * T, D_MODEL), F32)] + [st_shape(n) for n in halos]
    out_specs = [pl.BlockSpec((T, D_MODEL), tile_idx)] + [st_spec(n) for n in halos]

    scratch = [
        pltpu.VMEM((T, D_MODEL), BF16),
        pltpu.VMEM((3, T, D_MODEL), F32),
        pltpu.VMEM((T, D_MODEL), F32),
        pltpu.VMEM((T, D_MODEL), F32),
    ] + [pltpu.VMEM((SUBLANES * (n + Q), D_MODEL), F32) for n in halos] + [
        pltpu.VMEM((T, D_MODEL), BF16),
        pltpu.VMEM((T, D_MODEL), BF16),
        pltpu.VMEM((T, D_MODEL), BF16),
    ]
    if chained:
        scratch += [pltpu.VMEM((SUBLANES * n, D_MODEL), F32) for n in halos]
    if permute_in:
        scratch.append(pltpu.VMEM((T, D_MODEL), F32))

    return pl.pallas_call(
        functools.partial(_mixer_kernel, q_frames=Q, sps=sps, chained=chained,
                          permute_in=permute_in, pos0=pos0),
        grid=(n_seq, tiles_per_seq),
        in_specs=in_specs,
        out_specs=out_specs,
        out_shape=out_shape,
        scratch_shapes=scratch,
        compiler_params=pltpu.CompilerParams(
            dimension_semantics=("arbitrary", "arbitrary"),
            vmem_limit_bytes=VMEM_LIMIT_BYTES),
        name="mixer_chained" if chained else "mixer_streams",
    )(*args)


def _ffn_kernel(*refs, q_frames, final):
    Q = q_frames
    T = SUBLANES * Q
    n_blocks = T // ROWS
    x_ref, ada_ref, n2g_ref, wfi_ref, wfo_ref, fg_ref, o_ref = refs[:7]
    h_ref, act_ref = refs[7:9]

    n2g = n2g_ref[...]
    for i in range(n_blocks):
        rows = slice(i * ROWS, (i + 1) * ROWS)
        hrow = _rms(x_ref[rows, :]) * n2g * (1.0 + ada_ref[A_S2]) + ada_ref[A_SH2]
        h_ref[rows, :] = hrow.astype(BF16)

    h = h_ref[...]
    for f in range(D_FF // MXU_COLS):
        gate = _dot(h, wfi_ref[:, _colblk(f)])
        up = _dot(h, wfi_ref[:, _colblk(f, D_FF)])
        act_ref[:, _colblk(f)] = (_silu(gate) * up).astype(BF16)

    dst_ref = refs[9] if final else o_ref
    for n in range(N_COLBLK):
        cs = _colblk(n)
        y = _dot(act_ref[...], wfo_ref[:, cs])
        g2 = ada_ref[A_G2, :, cs]
        for i in range(n_blocks):
            rows = slice(i * ROWS, (i + 1) * ROWS)
            dst_ref[rows, cs] = x_ref[rows, cs] + g2 * y[rows, :]

    if final:
        xn_ref, ys_ref = refs[9:11]
        fg = fg_ref[...]
        def out_body(i):
            y = _rms(xn_ref[_blk(i), :]) * fg
            for c in range(N_CHUNKS):
                ys_ref[_at(c * T + i * ROWS, ROWS, ROWS), :] = y[:, c * LANES:(c + 1) * LANES]
        _loop(n_blocks, out_body, unroll=4)
        def unperm_body(i):
            r = i & (SUBLANES - 1)
            q0 = (i >> 3) * SUBLANES
            for c in range(N_CHUNKS):
                o_ref[_at(r * Q + q0, SUBLANES, SUBLANES), c * LANES:(c + 1) * LANES] = (
                    ys_ref[pl.ds(c * T + SUBLANES * q0 + r, SUBLANES, stride=SUBLANES), :])
        _loop(T // SUBLANES, unperm_body, unroll=2)


def _ffn_call(x, ada, n2g, wfi, wfo, fg, l, *, n_seq, tiles_per_seq, q_frames, final):
    Q = q_frames
    T = SUBLANES * Q
    n_tiles = n_seq * tiles_per_seq
    tile_idx = lambda b, j: (b * tiles_per_seq + j, 0)
    scratch = [
        pltpu.VMEM((T, D_MODEL), BF16),
        pltpu.VMEM((T, D_FF), BF16),
    ]
    if final:
        scratch += [pltpu.VMEM((T, D_MODEL), F32),
                    pltpu.VMEM((N_CHUNKS * T, LANES), F32)]
    return pl.pallas_call(
        functools.partial(_ffn_kernel, q_frames=Q, final=final),
        grid=(n_seq, tiles_per_seq),
        in_specs=[
            pl.BlockSpec((T, D_MODEL), tile_idx),
            pl.BlockSpec((None, 6, ROWS, D_MODEL), lambda b, j: (b, 0, 0, 0)),
            _layer_spec(n2g, l), _layer_spec(wfi, l), _layer_spec(wfo, l),
            pl.BlockSpec(fg.shape, lambda b, j: (0, 0), pipeline_mode=pl.Buffered(1)),
        ],
        out_specs=pl.BlockSpec((T, D_MODEL), tile_idx),
        out_shape=jax.ShapeDtypeStruct((n_tiles * T, D_MODEL), F32),
        scratch_shapes=scratch,
        compiler_params=pltpu.CompilerParams(
            dimension_semantics=("arbitrary", "arbitrary"),
            vmem_limit_bytes=VMEM_LIMIT_BYTES),
        name="ffn_final" if final else "ffn",
    )(x, ada, n2g, wfi, wfo, fg)


def _run_group(x, ada, hists, w_mixer, w_ffn, final_g, *, n_seq, tiles_per_seq,
               q_frames, sps, chained, pos0):
    states = []
    for l in range(DEPTH):
        x, sa, sb, sp = _mixer_call(
            x, ada[l], None if chained else hists[l], w_mixer, l,
            n_seq=n_seq, tiles_per_seq=tiles_per_seq, q_frames=q_frames, sps=sps,
            chained=chained, permute_in=(l == 0), pos0=pos0)
        x = _ffn_call(x, ada[l], *w_ffn, final_g, l, n_seq=n_seq,
                      tiles_per_seq=tiles_per_seq, q_frames=q_frames,
                      final=(l == DEPTH - 1))
        states.append((sa, sb, sp))
    return x, states


def kernel(x_prompt, x_sample, c_prompt, c_sample, cache_conv_a, cache_conv_b, cache_pool, w_ada, b_ada, norm1_g, w_in, w_conv_a, w_out_a, w_conv_b, b_conv_b, ln_b_g, ln_b_b, w_out_b, w_pool, pool_scale, w_o, norm2_g, w_ffn_in, w_ffn_out, final_g):
    bp, sp_len, _ = x_prompt.shape
    bs, ss_len, _ = x_sample.shape
    q_frames = 32
    tile = SUBLANES * q_frames
    assert q_frames >= K_B - 1 and sp_len % tile == 0 and ss_len % q_frames == 0
    sps_s = ss_len // q_frames
    seq_per_tile = SUBLANES // sps_s
    assert SUBLANES % sps_s == 0 and bs % seq_per_tile == 0
    n_sgrp = bs // seq_per_tile

    row = lambda v: v.reshape(DEPTH, 1, -1)
    w_mixer = [row(norm1_g), w_in.astype(BF16), w_conv_a, w_out_a.astype(BF16),
               w_conv_b, row(b_conv_b), row(ln_b_g), row(ln_b_b),
               w_out_b.astype(BF16), w_pool.astype(BF16), row(pool_scale),
               w_o.astype(BF16)]
    w_ffn = [row(norm2_g), w_ffn_in.astype(BF16), w_ffn_out.astype(BF16)]
    fg = final_g.reshape(1, D_MODEL)

    ada = _ada_call(jnp.concatenate([c_prompt, c_sample], axis=0), w_ada, b_ada)
    ada = ada.reshape(DEPTH, bp + bs, 6, D_MODEL)
    ada_p = jnp.broadcast_to(ada[:, :bp, :, None, :], (DEPTH, bp, 6, ROWS, D_MODEL))
    ada_s = ada[:, bp:].reshape(DEPTH, n_sgrp, seq_per_tile, 6, D_MODEL)
    ada_s = jnp.repeat(jnp.transpose(ada_s, (0, 1, 3, 2, 4)), sps_s, axis=3)
    ada_s = jnp.tile(ada_s, (1, 1, 1, ROWS // SUBLANES, 1))

    def frame_major(cache):
        n = cache.shape[2]
        c = cache.reshape(DEPTH, n_sgrp, seq_per_tile, n, D_MODEL)
        return jnp.repeat(jnp.transpose(c, (0, 1, 3, 2, 4)), sps_s, axis=3)
    hists = list(zip(frame_major(cache_conv_a), frame_major(cache_conv_b),
                     frame_major(cache_pool)))

    yp, st_p = _run_group(
        x_prompt.reshape(bp * sp_len, D_MODEL), ada_p, None, w_mixer, w_ffn, fg,
        n_seq=bp, tiles_per_seq=sp_len // tile, q_frames=q_frames,
        sps=SUBLANES, chained=True, pos0=0)
    ys, st_s = _run_group(
        x_sample.reshape(bs * ss_len, D_MODEL), ada_s, hists, w_mixer, w_ffn, fg,
        n_seq=n_sgrp, tiles_per_seq=1, q_frames=q_frames, sps=sps_s, chained=False,
        pos0=PAST_LEN)

    def seq_major(s):
        n = s.shape[1]
        s = s[:, :, sps_s - 1::sps_s, :]
        return jnp.transpose(s, (0, 2, 1, 3)).reshape(bs, n, D_MODEL)

    return (yp.reshape(bp, sp_len, D_MODEL), ys.reshape(bs, ss_len, D_MODEL),
            jnp.stack([s[0] for s in st_p]), jnp.stack([s[1] for s in st_p]),
            jnp.stack([s[2] for s in st_p]),
            jnp.stack([seq_major(s[0]) for s in st_s]),
            jnp.stack([seq_major(s[1]) for s in st_s]),
            jnp.stack([seq_major(s[2]) for s in st_s]))
```

```python
import functools

import jax
import jax.numpy as jnp
from jax import lax
from jax.experimental import pallas as pl
from jax.experimental.pallas import tpu as pltpu

F32 = jnp.float32
BF16 = jnp.bfloat16

D_MODEL = 1024
DEPTH = 2
EPS = 1e-6
K_A = 3
K_B = 31
POOL_WINDOWS = (2, 4, 8, 16)
N_POOL = len(POOL_WINDOWS)
POOL_GROUP = D_MODEL // N_POOL
POOL_PAD = max(POOL_WINDOWS) - 1
N_IN = 9 * D_MODEL
D_FF = 2816
PAST_LEN = 1024

SUBLANES = 8
LANES = 128
ROWS = 16
N_CHUNKS = D_MODEL // LANES
MXU_COLS = 256
N_COLBLK = D_MODEL // MXU_COLS
VMEM_LIMIT_BYTES = 60 * 1024 * 1024

C_BG, C_CG, C_HA, C_GA, C_GB, C_PIN, C_G0, C_G1, C_G2 = (i * D_MODEL for i in range(9))

A_SH1, A_S1, A_G1, A_SH2, A_S2, A_G2 = range(6)


def _sigmoid(x):
    return 0.5 * jnp.tanh(0.5 * x) + 0.5


def _silu(x):
    hx = 0.5 * x
    return hx + hx * jnp.tanh(hx)


def _dot(a, b):
    return jnp.dot(a, b, preferred_element_type=F32)


def _blk(i, n=ROWS):
    return pl.ds(pl.multiple_of(i * n, n), n)


def _at(start, n, align):
    return pl.ds(pl.multiple_of(start, align), n)


def _colblk(n, c0=0):
    return slice(c0 + n * MXU_COLS, c0 + (n + 1) * MXU_COLS)


def _loop(n, body, unroll=1):
    def step(i, carry):
        body(i)
        return carry
    lax.fori_loop(0, n, step, 0, unroll=unroll)


def _rms(x):
    return x * lax.rsqrt(jnp.mean(x * x, axis=-1, keepdims=True) + EPS)


def _ada_kernel(c_ref, w_ref, b_ref, o_ref):
    sc = _silu(c_ref[...]).astype(BF16)
    o_ref[...] = _dot(sc, w_ref[...].astype(BF16)) + b_ref[...]


def _ada_call(c_all, w_ada, b_ada):
    n = c_all.shape[0]
    tn = 1536
    return pl.pallas_call(
        _ada_kernel,
        grid=(DEPTH, 6 * D_MODEL // tn),
        in_specs=[
            pl.BlockSpec((n, D_MODEL), lambda l, j: (0, 0)),
            pl.BlockSpec((None, D_MODEL, tn), lambda l, j: (l, 0, j)),
            pl.BlockSpec((None, 1, tn), lambda l, j: (l, 0, j)),
        ],
        out_specs=pl.BlockSpec((None, n, tn), lambda l, j: (l, 0, j)),
        out_shape=jax.ShapeDtypeStruct((DEPTH, n, 6 * D_MODEL), F32),
        compiler_params=pltpu.CompilerParams(
            dimension_semantics=("arbitrary", "arbitrary"),
            vmem_limit_bytes=VMEM_LIMIT_BYTES),
        name="ada",
    )(c_all, w_ada, b_ada.reshape(DEPTH, 1, 6 * D_MODEL))


def _mixer_kernel(*refs, q_frames, sps, chained, permute_in, pos0):
    Q = q_frames
    T = SUBLANES * Q
    refs = list(refs)
    n_x = N_CHUNKS if permute_in else 1
    x_refs = refs[:n_x]
    xn_refs = refs[n_x:2 * n_x]
    ada_ref, adan_ref = refs[2 * n_x:2 * n_x + 2]
    refs = refs[2 * n_x + 2:]
    if not chained:
        hist_a_ref, hist_b_ref, hist_p_ref = refs[:3]
        refs = refs[3:]
    else:
        hist_a_ref = hist_b_ref = hist_p_ref = None
    (n1g_ref, win_ref, wca_ref, woa_ref, wcb_ref, bcb_ref, lng_ref, lnb_ref,
     wob_ref, wpool_ref, pscale_ref, wo_ref) = refs[:12]
    refs = refs[12:]
    xo_ref, sa_ref, sb_ref, sp_ref = refs[:4]
    refs = refs[4:]
    (h_ref, hn_ref, mb_ref, sg_ref, m_ref, cb_ref, eu_ref, ev_ref, ep_ref,
     a_ref, b_ref) = refs[:11]
    refs = refs[11:]
    if chained:
        cu_ref, cv_ref, cp_ref = refs[:3]
        refs = refs[3:]
    else:
        cu_ref = cv_ref = cp_ref = None
    if permute_in:
        xs_ref, xsn_ref = refs
    else:
        xs_ref, xsn_ref = x_refs[0], xn_refs[0]

    j = pl.program_id(1)
    n_blocks = T // ROWS
    n1g = n1g_ref[...]

    def prepare(src_refs, mod_ref, xs_dst, h_dst):
        if permute_in:
            for q in range(Q):
                rows = slice(q * SUBLANES, (q + 1) * SUBLANES)
                for c in range(N_CHUNKS):
                    xs_dst[rows, c * LANES:(c + 1) * LANES] = (
                        src_refs[c][pl.ds(q, SUBLANES, stride=Q), :])
        for i in range(n_blocks):
            rows = slice(i * ROWS, (i + 1) * ROWS)
            hrow = _rms(xs_dst[rows, :]) * n1g * (1.0 + mod_ref[A_S1]) + mod_ref[A_SH1]
            h_dst[rows, :] = hrow.astype(BF16)

    first_step = (pl.program_id(0) == 0) & (j == 0)

    @pl.when(first_step)
    def _():
        prepare(x_refs, ada_ref, xs_ref, h_ref)

    @pl.when(jnp.logical_not(first_step))
    def _():
        h_ref[...] = hn_ref[...]
        if permute_in:
            xs_ref[...] = xsn_ref[...]

    if chained:
        @pl.when(j == 0)
        def _():
            cu_ref[...] = jnp.zeros_like(cu_ref)
            cv_ref[...] = jnp.zeros_like(cv_ref)
            cp_ref[...] = jnp.zeros_like(cp_ref)

    stream = lax.broadcasted_iota(jnp.int32, (SUBLANES, MXU_COLS), 0)
    first = (stream & (sps - 1)) == 0

    def fill_halo(e_ref, carry_ref, hist_ref, n_halo, cs):
        for g in range(n_halo):
            dst = slice(SUBLANES * g, SUBLANES * (g + 1))
            tail = e_ref[SUBLANES * (Q + g):SUBLANES * (Q + g + 1), cs]
            prev = pltpu.roll(tail, 1, axis=0)
            hist = carry_ref[dst, cs] if chained else hist_ref[g, :, cs]
            e_ref[dst, cs] = jnp.where(first, hist, prev)
            if chained:
                carry_ref[dst, cs] = prev

    prepare(xn_refs, adan_ref, xsn_ref, hn_ref)

    if pos0 < POOL_PAD:
        rr = lax.broadcasted_iota(jnp.int32, (T, 1), 0)
        frame = j * T + (rr & (SUBLANES - 1)) * Q + (rr >> 3)
        avail = pos0 + frame + 1

    h = h_ref[...]
    pscale = pscale_ref[...]
    for n in range(N_COLBLK):
        cs = _colblk(n)
        proj = lambda c0: _dot(h, win_ref[:, _colblk(n, c0)])

        eu_ref[SUBLANES * (K_A - 1):, cs] = proj(C_CG) * proj(C_HA)
        fill_halo(eu_ref, cu_ref, hist_a_ref, K_A - 1, cs)
        acc = None
        for k in range(K_A):
            t = eu_ref[SUBLANES * k:SUBLANES * k + T, cs] * wca_ref[k:k + 1, cs]
            acc = t if acc is None else acc + t
        a_ref[:, cs] = (proj(C_BG) * acc).astype(BF16)

        ev_ref[SUBLANES * (K_B - 1):, cs] = proj(C_GA) * _sigmoid(proj(C_GB))
        fill_halo(ev_ref, cv_ref, hist_b_ref, K_B - 1, cs)
        conv_rows = 64
        for cc in range(MXU_COLS // LANES):
            col = slice(cs.start + cc * LANES, cs.start + (cc + 1) * LANES)
            for rb in range(T // conv_rows):
                r0 = rb * conv_rows
                acc = jnp.broadcast_to(bcb_ref[:, col], (conv_rows, LANES))
                for k in range(K_B):
                    win = ev_ref[r0 + SUBLANES * k:r0 + SUBLANES * k + conv_rows, col]
                    acc = acc + win * wcb_ref[k:k + 1, col]
                cb_ref[r0:r0 + conv_rows, col] = acc

        for gi, c0 in enumerate((C_G0, C_G1, C_G2)):
            sg_ref[gi, :, cs] = _sigmoid(proj(c0))

        ep_ref[SUBLANES * POOL_PAD:, cs] = proj(C_PIN)
        fill_halo(ep_ref, cp_ref, hist_p_ref, POOL_PAD, cs)
        w = POOL_WINDOWS[n]
        cur = ep_ref[SUBLANES * POOL_PAD:, cs]
        acc = cur
        for s in range(1, w):
            acc = acc + ep_ref[SUBLANES * (POOL_PAD - s):SUBLANES * (POOL_PAD - s) + T, cs]
        if pos0 < POOL_PAD:
            inv = 1.0 / jnp.minimum(avail, w).astype(F32)
        else:
            inv = 1.0 / w
        pooled = (acc * inv - cur).astype(BF16)
        y_c = _dot(pooled, wpool_ref[n])
        m_ref[:, cs] = sg_ref[2, :, cs] * (y_c * pscale[:, cs])

    for n in range(N_COLBLK):
        cs = _colblk(n)
        m_ref[:, cs] += sg_ref[0, :, cs] * _dot(a_ref[...], woa_ref[:, cs])

    lng = lng_ref[...]
    lnb = lnb_ref[...]
    for i in range(n_blocks):
        rows = slice(i * ROWS, (i + 1) * ROWS)
        cb = cb_ref[rows, :]
        mu = jnp.mean(cb, axis=-1, keepdims=True)
        d = cb - mu
        var = jnp.mean(d * d, axis=-1, keepdims=True)
        y = d * lax.rsqrt(var + EPS) * lng + lnb
        b_ref[rows, :] = _silu(y).astype(BF16)

    for n in range(N_COLBLK):
        cs = _colblk(n)
        m = m_ref[:, cs] + sg_ref[1, :, cs] * _dot(b_ref[...], wob_ref[:, cs])
        mb_ref[:, cs] = m.astype(BF16)

    for n in range(N_COLBLK):
        cs = _colblk(n)
        out = _dot(mb_ref[...], wo_ref[:, cs])
        g1 = ada_ref[A_G1, :, cs]
        for i in range(n_blocks):
            rows = slice(i * ROWS, (i + 1) * ROWS)
            xo_ref[rows, cs] = xs_ref[rows, cs] + g1 * out[rows, :]

    if chained:
        @pl.when(j == pl.num_programs(1) - 1)
        def _():
            for carry_ref, s_ref, n_halo in ((cu_ref, sa_ref, K_A - 1),
                                             (cv_ref, sb_ref, K_B - 1),
                                             (cp_ref, sp_ref, POOL_PAD)):
                for g in range(n_halo):
                    s_ref[g:g + 1, :] = carry_ref[SUBLANES * g:SUBLANES * g + 1, :]
    else:
        for e_ref, s_ref, n_halo in ((eu_ref, sa_ref, K_A - 1),
                                     (ev_ref, sb_ref, K_B - 1),
                                     (ep_ref, sp_ref, POOL_PAD)):
            for s in range(SUBLANES // sps):
                r = sps * s + sps - 1
                for g in range(n_halo):
                    row = SUBLANES * (Q + g) + r
                    s_ref[s, g:g + 1, :] = e_ref[row:row + 1, :]


def _layer_spec(arr, l):
    nd = arr.ndim - 1
    return pl.BlockSpec((None,) + arr.shape[1:], lambda *_: (l,) + (0,) * nd,
                        pipeline_mode=pl.Buffered(1))


def _mixer_call(x, ada, hists, w, l, *, n_seq, tiles_per_seq, q_frames, sps, chained,
                permute_in, pos0):
    Q = q_frames
    T = SUBLANES * Q
    n_tiles = n_seq * tiles_per_seq
    tile_of = lambda b, j: b * tiles_per_seq + j
    next_of = lambda b, j: jnp.minimum(tile_of(b, j) + 1, n_tiles - 1)
    tile_idx = lambda b, j: (tile_of(b, j), 0)
    in_specs, args = [], []
    for idx in (tile_of, next_of):
        if permute_in:
            in_specs += [pl.BlockSpec((T, LANES), lambda b, j, c=c, idx=idx: (idx(b, j), c))
                         for c in range(N_CHUNKS)]
            args += [x] * N_CHUNKS
        else:
            in_specs.append(pl.BlockSpec((T, D_MODEL), lambda b, j, idx=idx: (idx(b, j), 0)))
            args.append(x)
    for idx in (tile_of, next_of):
        in_specs.append(pl.BlockSpec(
            (None, 6, ROWS, D_MODEL),
            lambda b, j, idx=idx: (idx(b, j) // tiles_per_seq, 0, 0, 0)))
        args.append(ada)
    if not chained:
        for hst in hists:
            in_specs.append(pl.BlockSpec((None,) + hst.shape[1:], lambda b, j: (b, 0, 0, 0)))
            args.append(hst)
    for arr in w:
        in_specs.append(_layer_spec(arr, l))
        args.append(arr)

    if chained:
        st_shape = lambda n: jax.ShapeDtypeStruct((n_seq, n, D_MODEL), F32)
        st_spec = lambda n: pl.BlockSpec((None, n, D_MODEL), lambda b, j: (b, 0, 0))
    else:
        spt = SUBLANES // sps
        st_shape = lambda n: jax.ShapeDtypeStruct((n_seq, spt, n, D_MODEL), F32)
        st_spec = lambda n: pl.BlockSpec((None, spt, n, D_MODEL), lambda b, j: (b, 0, 0, 0))
    halos = (K_A - 1, K_B - 1, POOL_PAD)
    out_shape = [jax.ShapeDtypeStruct((n_tiles * T, D_MODEL), F32)] + [st_shape(n) for n in halos]
    out_specs = [pl.BlockSpec((T, D_MODEL), tile_idx)] + [st_spec(n) for n in halos]

    scratch = [
        pltpu.VMEM((T, D_MODEL), BF16),
        pltpu.VMEM((T, D_MODEL), BF16),
        pltpu.VMEM((T, D_MODEL), BF16),
        pltpu.VMEM((3, T, D_MODEL), F32),
        pltpu.VMEM((T, D_MODEL), F32),
        pltpu.VMEM((T, D_MODEL), F32),
    ] + [pltpu.VMEM((SUBLANES * (n + Q), D_MODEL), F32) for n in halos] + [
        pltpu.VMEM((T, D_MODEL), BF16),
        pltpu.VMEM((T, D_MODEL), BF16),
    ]
    if chained:
        scratch += [pltpu.VMEM((SUBLANES * n, D_MODEL), F32) for n in halos]
    if permute_in:
        scratch += [pltpu.VMEM((T, D_MODEL), F32)] * 2

    return pl.pallas_call(
        functools.partial(_mixer_kernel, q_frames=Q, sps=sps, chained=chained,
                          permute_in=permute_in, pos0=pos0),
        grid=(n_seq, tiles_per_seq),
        in_specs=in_specs,
        out_specs=out_specs,
        out_shape=out_shape,
        scratch_shapes=scratch,
        compiler_params=pltpu.CompilerParams(
            dimension_semantics=("arbitrary", "arbitrary"),
            vmem_limit_bytes=VMEM_LIMIT_BYTES),
        name="mixer_chained" if chained else "mixer_streams",
    )(*args)


def _ffn_kernel(*refs, q_frames, final):
    Q = q_frames
    T = SUBLANES * Q
    n_blocks = T // ROWS
    (x_ref, xnext_ref, ada_ref, adan_ref, n2g_ref, wfi_ref, wfo_ref, fg_ref,
     o_ref) = refs[:9]
    h_ref, hn_ref, act_ref = refs[9:12]

    n2g = n2g_ref[...]
    def prepare_rows(src_ref, mod_ref, h_dst, i):
        rows = slice(i * ROWS, (i + 1) * ROWS)
        hrow = _rms(src_ref[rows, :]) * n2g * (1.0 + mod_ref[A_S2]) + mod_ref[A_SH2]
        h_dst[rows, :] = hrow.astype(BF16)

    first_step = (pl.program_id(0) == 0) & (pl.program_id(1) == 0)

    @pl.when(first_step)
    def _():
        for i in range(n_blocks):
            prepare_rows(x_ref, ada_ref, h_ref, i)

    @pl.when(jnp.logical_not(first_step))
    def _():
        h_ref[...] = hn_ref[...]

    h = h_ref[...]
    n_f = D_FF // MXU_COLS
    for f in range(n_f):
        gate = _dot(h, wfi_ref[:, _colblk(f)])
        up = _dot(h, wfi_ref[:, _colblk(f, D_FF)])
        act_ref[:, _colblk(f)] = (_silu(gate) * up).astype(BF16)
        for i in range(f * n_blocks // n_f, (f + 1) * n_blocks // n_f):
            prepare_rows(xnext_ref, adan_ref, hn_ref, i)

    dst_ref = refs[12] if final else o_ref
    for n in range(N_COLBLK):
        cs = _colblk(n)
        y = _dot(act_ref[...], wfo_ref[:, cs])
        g2 = ada_ref[A_G2, :, cs]
        for i in range(n_blocks):
            rows = slice(i * ROWS, (i + 1) * ROWS)
            dst_ref[rows, cs] = x_ref[rows, cs] + g2 * y[rows, :]

    if final:
        xn_ref, ys_ref = refs[12:14]
        fg = fg_ref[...]
        def out_body(i):
            y = _rms(xn_ref[_blk(i), :]) * fg
            for c in range(N_CHUNKS):
                ys_ref[_at(c * T + i * ROWS, ROWS, ROWS), :] = y[:, c * LANES:(c + 1) * LANES]
        _loop(n_blocks, out_body, unroll=4)
        def unperm_body(i):
            r = i & (SUBLANES - 1)
            q0 = (i >> 3) * SUBLANES
            for c in range(N_CHUNKS):
                o_ref[_at(r * Q + q0, SUBLANES, SUBLANES), c * LANES:(c + 1) * LANES] = (
                    ys_ref[pl.ds(c * T + SUBLANES * q0 + r, SUBLANES, stride=SUBLANES), :])
        _loop(T // SUBLANES, unperm_body, unroll=2)


def _ffn_call(x, ada, n2g, wfi, wfo, fg, l, *, n_seq, tiles_per_seq, q_frames, final):
    Q = q_frames
    T = SUBLANES * Q
    n_tiles = n_seq * tiles_per_seq
    tile_of = lambda b, j: b * tiles_per_seq + j
    next_of = lambda b, j: jnp.minimum(tile_of(b, j) + 1, n_tiles - 1)
    tile_idx = lambda b, j: (tile_of(b, j), 0)
    ada_spec = lambda idx: pl.BlockSpec(
        (None, 6, ROWS, D_MODEL), lambda b, j: (idx(b, j) // tiles_per_seq, 0, 0, 0))
    scratch = [
        pltpu.VMEM((T, D_MODEL), BF16),
        pltpu.VMEM((T, D_MODEL), BF16),
        pltpu.VMEM((T, D_FF), BF16),
    ]
    if final:
        scratch += [pltpu.VMEM((T, D_MODEL), F32),
                    pltpu.VMEM((N_CHUNKS * T, LANES), F32)]
    return pl.pallas_call(
        functools.partial(_ffn_kernel, q_frames=Q, final=final),
        grid=(n_seq, tiles_per_seq),
        in_specs=[
            pl.BlockSpec((T, D_MODEL), tile_idx),
            pl.BlockSpec((T, D_MODEL), lambda b, j: (next_of(b, j), 0)),
            ada_spec(tile_of), ada_spec(next_of),
            _layer_spec(n2g, l), _layer_spec(wfi, l), _layer_spec(wfo, l),
            pl.BlockSpec(fg.shape, lambda b, j: (0, 0), pipeline_mode=pl.Buffered(1)),
        ],
        out_specs=pl.BlockSpec((T, D_MODEL), tile_idx),
        out_shape=jax.ShapeDtypeStruct((n_tiles * T, D_MODEL), F32),
        scratch_shapes=scratch,
        compiler_params=pltpu.CompilerParams(
            dimension_semantics=("arbitrary", "arbitrary"),
            vmem_limit_bytes=VMEM_LIMIT_BYTES),
        name="ffn_final" if final else "ffn",
    )(x, x, ada, ada, n2g, wfi, wfo, fg)


def _run_group(x, ada, hists, w_mixer, w_ffn, final_g, *, n_seq, tiles_per_seq,
               q_frames, sps, chained, pos0):
    states = []
    for l in range(DEPTH):
        x, sa, sb, sp = _mixer_call(
            x, ada[l], None if chained else hists[l], w_mixer, l,
            n_seq=n_seq, tiles_per_seq=tiles_per_seq, q_frames=q_frames, sps=sps,
            chained=chained, permute_in=(l == 0), pos0=pos0)
        x = _ffn_call(x, ada[l], *w_ffn, final_g, l, n_seq=n_seq,
                      tiles_per_seq=tiles_per_seq, q_frames=q_frames,
                      final=(l == DEPTH - 1))
        states.append((sa, sb, sp))
    return x, states


def kernel(x_prompt, x_sample, c_prompt, c_sample, cache_conv_a, cache_conv_b, cache_pool, w_ada, b_ada, norm1_g, w_in, w_conv_a, w_out_a, w_conv_b, b_conv_b, ln_b_g, ln_b_b, w_out_b, w_pool, pool_scale, w_o, norm2_g, w_ffn_in, w_ffn_out, final_g):
    bp, sp_len, _ = x_prompt.shape
    bs, ss_len, _ = x_sample.shape
    q_frames = 32
    tile = SUBLANES * q_frames
    assert q_frames >= K_B - 1 and sp_len % tile == 0 and ss_len % q_frames == 0
    sps_s = ss_len // q_frames
    seq_per_tile = SUBLANES // sps_s
    assert SUBLANES % sps_s == 0 and bs % seq_per_tile == 0
    n_sgrp = bs // seq_per_tile

    row = lambda v: v.reshape(DEPTH, 1, -1)
    w_mixer = [row(norm1_g), w_in.astype(BF16), w_conv_a, w_out_a.astype(BF16),
               w_conv_b, row(b_conv_b), row(ln_b_g), row(ln_b_b),
               w_out_b.astype(BF16), w_pool.astype(BF16), row(pool_scale),
               w_o.astype(BF16)]
    w_ffn = [row(norm2_g), w_ffn_in.astype(BF16), w_ffn_out.astype(BF16)]
    fg = final_g.reshape(1, D_MODEL)

    ada = _ada_call(jnp.concatenate([c_prompt, c_sample], axis=0), w_ada, b_ada)
    ada = ada.reshape(DEPTH, bp + bs, 6, D_MODEL)
    ada_p = jnp.broadcast_to(ada[:, :bp, :, None, :], (DEPTH, bp, 6, ROWS, D_MODEL))
    ada_s = ada[:, bp:].reshape(DEPTH, n_sgrp, seq_per_tile, 6, D_MODEL)
    ada_s = jnp.repeat(jnp.transpose(ada_s, (0, 1, 3, 2, 4)), sps_s, axis=3)
    ada_s = jnp.tile(ada_s, (1, 1, 1, ROWS // SUBLANES, 1))

    def frame_major(cache):
        n = cache.shape[2]
        c = cache.reshape(DEPTH, n_sgrp, seq_per_tile, n, D_MODEL)
        return jnp.repeat(jnp.transpose(c, (0, 1, 3, 2, 4)), sps_s, axis=3)
    hists = list(zip(frame_major(cache_conv_a), frame_major(cache_conv_b),
                     frame_major(cache_pool)))

    yp, st_p = _run_group(
        x_prompt.reshape(bp * sp_len, D_MODEL), ada_p, None, w_mixer, w_ffn, fg,
        n_seq=bp, tiles_per_seq=sp_len // tile, q_frames=q_frames,
        sps=SUBLANES, chained=True, pos0=0)
    ys, st_s = _run_group(
        x_sample.reshape(bs * ss_len, D_MODEL), ada_s, hists, w_mixer, w_ffn, fg,
        n_seq=n_sgrp, tiles_per_seq=1, q_frames=q_frames, sps=sps_s, chained=False,
        pos0=PAST_LEN)

    def seq_major(s):
        return s.reshape(bs, s.shape[2], D_MODEL)

    return (yp.reshape(bp, sp_len, D_MODEL), ys.reshape(bs, ss_len, D_MODEL),
            jnp.stack([s[0] for s in st_p]), jnp.stack([s[1] for s in st_p]),
            jnp.stack([s[2] for s in st_p]),
            jnp.stack([seq_major(s[0]) for s in st_s]),
            jnp.stack([seq_major(s[1]) for s in st_s]),
            jnp.stack([seq_major(s[2]) for s in st_s]))
```

```python
import functools

import jax
import jax.numpy as jnp
from jax import lax
from jax.experimental import pallas as pl
from jax.experimental.pallas import tpu as pltpu

F32 = jnp.float32
BF16 = jnp.bfloat16

D_MODEL = 1024
DEPTH = 2
EPS = 1e-6
K_A = 3
K_B = 31
POOL_WINDOWS = (2, 4, 8, 16)
N_POOL = len(POOL_WINDOWS)
POOL_GROUP = D_MODEL // N_POOL
POOL_PAD = max(POOL_WINDOWS) - 1
N_IN = 9 * D_MODEL
D_FF = 2816
PAST_LEN = 1024

SUBLANES = 8
LANES = 128
ROWS = 16
N_CHUNKS = D_MODEL // LANES
MXU_COLS = 256
N_COLBLK = D_MODEL // MXU_COLS
CONV_CHAIN = 16
VMEM_LIMIT_BYTES = 60 * 1024 * 1024

C_BG, C_CG, C_HA, C_GA, C_GB, C_PIN, C_G0, C_G1, C_G2 = (i * D_MODEL for i in range(9))

A_SH1, A_S1, A_G1, A_SH2, A_S2, A_G2 = range(6)


def _sigmoid(x):
    return 0.5 * jnp.tanh(0.5 * x) + 0.5


def _silu(x):
    hx = 0.5 * x
    return hx + hx * jnp.tanh(hx)


def _dot(a, b):
    return jnp.dot(a, b, preferred_element_type=F32)


def _blk(i, n=ROWS):
    return pl.ds(pl.multiple_of(i * n, n), n)


def _at(start, n, align):
    return pl.ds(pl.multiple_of(start, align), n)


def _colblk(n, c0=0):
    return slice(c0 + n * MXU_COLS, c0 + (n + 1) * MXU_COLS)


def _loop(n, body, unroll=1):
    def step(i, carry):
        body(i)
        return carry
    lax.fori_loop(0, n, step, 0, unroll=unroll)


def _rms(x):
    return x * lax.rsqrt(jnp.mean(x * x, axis=-1, keepdims=True) + EPS)


def _ada_kernel(c_ref, w_ref, b_ref, o_ref):
    sc = _silu(c_ref[...]).astype(BF16)
    o_ref[...] = _dot(sc, w_ref[...].astype(BF16)) + b_ref[...]


def _ada_call(c_all, w_ada, b_ada):
    n = c_all.shape[0]
    tn = 1536
    return pl.pallas_call(
        _ada_kernel,
        grid=(DEPTH, 6 * D_MODEL // tn),
        in_specs=[
            pl.BlockSpec((n, D_MODEL), lambda l, j: (0, 0)),
            pl.BlockSpec((None, D_MODEL, tn), lambda l, j: (l, 0, j)),
            pl.BlockSpec((None, 1, tn), lambda l, j: (l, 0, j)),
        ],
        out_specs=pl.BlockSpec((None, n, tn), lambda l, j: (l, 0, j)),
        out_shape=jax.ShapeDtypeStruct((DEPTH, n, 6 * D_MODEL), F32),
        compiler_params=pltpu.CompilerParams(
            dimension_semantics=("arbitrary", "arbitrary"),
            vmem_limit_bytes=VMEM_LIMIT_BYTES),
        name="ada",
    )(c_all, w_ada, b_ada.reshape(DEPTH, 1, 6 * D_MODEL))


def _mixer_kernel(*refs, q_frames, sps, chained, permute_in, pos0):
    Q = q_frames
    T = SUBLANES * Q
    refs = list(refs)
    n_x = N_CHUNKS if permute_in else 1
    x_refs = refs[:n_x]
    ada_ref = refs[n_x]
    refs = refs[n_x + 1:]
    if not chained:
        hist_a_ref, hist_b_ref, hist_p_ref = refs[:3]
        refs = refs[3:]
    else:
        hist_a_ref = hist_b_ref = hist_p_ref = None
    (n1g_ref, win_ref, wca_ref, woa_ref, wcb_ref, bcb_ref, lng_ref, lnb_ref,
     wob_ref, wpool_ref, pscale_ref, wo_ref) = refs[:12]
    refs = refs[12:]
    xo_ref, sa_ref, sb_ref, sp_ref = refs[:4]
    refs = refs[4:]
    h_ref, sg_ref, m_ref, cb_ref, eu_ref, ev_ref, ep_ref, a_ref, b_ref = refs[:9]
    refs = refs[9:]
    if chained:
        cu_ref, cv_ref, cp_ref = refs[:3]
        refs = refs[3:]
    else:
        cu_ref = cv_ref = cp_ref = None
    xs_ref = refs[0] if permute_in else x_refs[0]

    j = pl.program_id(1)
    n_blocks = T // ROWS

    if permute_in:
        def perm_body(q):
            for c in range(N_CHUNKS):
                xs_ref[_blk(q, SUBLANES), c * LANES:(c + 1) * LANES] = (
                    x_refs[c][pl.ds(q, SUBLANES, stride=Q), :])
        _loop(Q, perm_body, unroll=2)

    if chained:
        @pl.when(j == 0)
        def _():
            cu_ref[...] = jnp.zeros_like(cu_ref)
            cv_ref[...] = jnp.zeros_like(cv_ref)
            cp_ref[...] = jnp.zeros_like(cp_ref)

    stream = lax.broadcasted_iota(jnp.int32, (SUBLANES, MXU_COLS), 0)
    first = (stream & (sps - 1)) == 0

    def fill_halo(e_ref, carry_ref, hist_ref, n_halo, cs):
        for g in range(n_halo):
            dst = slice(SUBLANES * g, SUBLANES * (g + 1))
            tail = e_ref[SUBLANES * (Q + g):SUBLANES * (Q + g + 1), cs]
            prev = pltpu.roll(tail, 1, axis=0)
            hist = carry_ref[dst, cs] if chained else hist_ref[g, :, cs]
            e_ref[dst, cs] = jnp.where(first, hist, prev)
            if chained:
                carry_ref[dst, cs] = prev

    n1g = n1g_ref[...]
    for i in range(n_blocks):
        rows = slice(i * ROWS, (i + 1) * ROWS)
        hrow = _rms(xs_ref[rows, :]) * n1g * (1.0 + ada_ref[A_S1]) + ada_ref[A_SH1]
        h_ref[rows, :] = hrow.astype(BF16)

    if pos0 < POOL_PAD:
        rr = lax.broadcasted_iota(jnp.int32, (T, 1), 0)
        frame = j * T + (rr & (SUBLANES - 1)) * Q + (rr >> 3)
        avail = pos0 + frame + 1

    h = h_ref[...]
    pscale = pscale_ref[...]
    for n in range(N_COLBLK):
        cs = _colblk(n)
        proj = lambda c0: _dot(h, win_ref[:, _colblk(n, c0)])

        eu_ref[SUBLANES * (K_A - 1):, cs] = proj(C_CG) * proj(C_HA)
        fill_halo(eu_ref, cu_ref, hist_a_ref, K_A - 1, cs)
        acc = None
        for k in range(K_A):
            t = eu_ref[SUBLANES * k:SUBLANES * k + T, cs] * wca_ref[k:k + 1, cs]
            acc = t if acc is None else acc + t
        a_ref[:, cs] = (proj(C_BG) * acc).astype(BF16)

        ev_ref[SUBLANES * (K_B - 1):, cs] = proj(C_GA) * _sigmoid(proj(C_GB))
        fill_halo(ev_ref, cv_ref, hist_b_ref, K_B - 1, cs)
        for cc in range(MXU_COLS // LANES):
            col = slice(cs.start + cc * LANES, cs.start + (cc + 1) * LANES)
            wk = [jnp.broadcast_to(wcb_ref[k:k + 1, col].astype(BF16), (ROWS, LANES))
                  for k in range(K_B)]
            n_win = 2 * (Q // 2 - 1) + K_B
            win = [ev_ref[SUBLANES * g:SUBLANES * g + ROWS, col].astype(BF16)
                   for g in range(n_win)]
            bias = jnp.broadcast_to(bcb_ref[:, col], (ROWS, LANES))
            for i in range(Q // 2):
                acc = bias
                for k0 in range(0, K_B, CONV_CHAIN):
                    run = None
                    for k in range(k0, min(k0 + CONV_CHAIN, K_B)):
                        t = win[2 * i + k] * wk[k]
                        run = t if run is None else run + t
                    acc = acc + run.astype(F32)
                cb_ref[i * ROWS:(i + 1) * ROWS, col] = acc

        for gi, c0 in enumerate((C_G0, C_G1, C_G2)):
            sg_ref[gi, :, cs] = _sigmoid(proj(c0))

        ep_ref[SUBLANES * POOL_PAD:, cs] = proj(C_PIN)
        fill_halo(ep_ref, cp_ref, hist_p_ref, POOL_PAD, cs)
        w = POOL_WINDOWS[n]
        cur = ep_ref[SUBLANES * POOL_PAD:, cs]
        acc = cur
        for s in range(1, w):
            acc = acc + ep_ref[SUBLANES * (POOL_PAD - s):SUBLANES * (POOL_PAD - s) + T, cs]
        if pos0 < POOL_PAD:
            inv = 1.0 / jnp.minimum(avail, w).astype(F32)
        else:
            inv = 1.0 / w
        pooled = (acc * inv - cur).astype(BF16)
        y_c = _dot(pooled, wpool_ref[n])
        m_ref[:, cs] = sg_ref[2, :, cs] * (y_c * pscale[:, cs])

    for n in range(N_COLBLK):
        cs = _colblk(n)
        m_ref[:, cs] += sg_ref[0, :, cs] * _dot(a_ref[...], woa_ref[:, cs])

    lng = lng_ref[...]
    lnb = lnb_ref[...]
    for i in range(n_blocks):
        rows = slice(i * ROWS, (i + 1) * ROWS)
        cb = cb_ref[rows, :]
        mu = jnp.mean(cb, axis=-1, keepdims=True)
        d = cb - mu
        var = jnp.mean(d * d, axis=-1, keepdims=True)
        y = d * lax.rsqrt(var + EPS) * lng + lnb
        b_ref[rows, :] = _silu(y).astype(BF16)

    for n in range(N_COLBLK):
        cs = _colblk(n)
        m = m_ref[:, cs] + sg_ref[1, :, cs] * _dot(b_ref[...], wob_ref[:, cs])
        h_ref[:, cs] = m.astype(BF16)

    for n in range(N_COLBLK):
        cs = _colblk(n)
        out = _dot(h_ref[...], wo_ref[:, cs])
        g1 = ada_ref[A_G1, :, cs]
        for i in range(n_blocks):
            rows = slice(i * ROWS, (i + 1) * ROWS)
            xo_ref[rows, cs] = xs_ref[rows, cs] + g1 * out[rows, :]

    if chained:
        @pl.when(j == pl.num_programs(1) - 1)
        def _():
            for carry_ref, s_ref, n_halo in ((cu_ref, sa_ref, K_A - 1),
                                             (cv_ref, sb_ref, K_B - 1),
                                             (cp_ref, sp_ref, POOL_PAD)):
                for g in range(n_halo):
                    s_ref[g:g + 1, :] = carry_ref[SUBLANES * g:SUBLANES * g + 1, :]
    else:
        for e_ref, s_ref, n_halo in ((eu_ref, sa_ref, K_A - 1),
                                     (ev_ref, sb_ref, K_B - 1),
                                     (ep_ref, sp_ref, POOL_PAD)):
            for s in range(SUBLANES // sps):
                r = sps * s + sps - 1
                for g in range(n_halo):
                    row = SUBLANES * (Q + g) + r
                    s_ref[s, g:g + 1, :] = e_ref[row:row + 1, :]


def _layer_spec(arr, l):
    nd = arr.ndim - 1
    return pl.BlockSpec((None,) + arr.shape[1:], lambda *_: (l,) + (0,) * nd,
                        pipeline_mode=pl.Buffered(1))


def _mixer_call(x, ada, hists, w, l, *, n_seq, tiles_per_seq, q_frames, sps, chained,
                permute_in, pos0):
    Q = q_frames
    T = SUBLANES * Q
    n_tiles = n_seq * tiles_per_seq
    tile_idx = lambda b, j: (b * tiles_per_seq + j, 0)
    if permute_in:
        in_specs = [pl.BlockSpec((T, LANES), lambda b, j, c=c: (b * tiles_per_seq + j, c))
                    for c in range(N_CHUNKS)]
        args = [x] * N_CHUNKS
    else:
        in_specs = [pl.BlockSpec((T, D_MODEL), tile_idx)]
        args = [x]
    in_specs.append(pl.BlockSpec((None, 6, ROWS, D_MODEL), lambda b, j: (b, 0, 0, 0)))
    args.append(ada)
    if not chained:
        for hst in hists:
            in_specs.append(pl.BlockSpec((None,) + hst.shape[1:], lambda b, j: (b, 0, 0, 0)))
            args.append(hst)
    for arr in w:
        in_specs.append(_layer_spec(arr, l))
        args.append(arr)

    if chained:
        st_shape = lambda n: jax.ShapeDtypeStruct((n_seq, n, D_MODEL), F32)
        st_spec = lambda n: pl.BlockSpec((None, n, D_MODEL), lambda b, j: (b, 0, 0))
    else:
        spt = SUBLANES // sps
        st_shape = lambda n: jax.ShapeDtypeStruct((n_seq, spt, n, D_MODEL), F32)
        st_spec = lambda n: pl.BlockSpec((None, spt, n, D_MODEL), lambda b, j: (b, 0, 0, 0))
    halos = (K_A - 1, K_B - 1, POOL_PAD)
    out_shape = [jax.ShapeDtypeStruct((n_tiles * T, D_MODEL), F32)] + [st_shape(n) for n in halos]
    out_specs = [pl.BlockSpec((T, D_MODEL), tile_idx)] + [st_spec(n) for n in halos]

    scratch = [
        pltpu.VMEM((T, D_MODEL), BF16),
        pltpu.VMEM((3, T, D_MODEL), F32),
        pltpu.VMEM((T, D_MODEL), F32),
        pltpu.VMEM((T, D_MODEL), F32),
    ] + [pltpu.VMEM((SUBLANES * (n + Q), D_MODEL), F32) for n in halos] + [
        pltpu.VMEM((T, D_MODEL), BF16),
        pltpu.VMEM((T, D_MODEL), BF16),
    ]
    if chained:
        scratch += [pltpu.VMEM((SUBLANES * n, D_MODEL), F32) for n in halos]
    if permute_in:
        scratch.append(pltpu.VMEM((T, D_MODEL), F32))

    return pl.pallas_call(
        functools.partial(_mixer_kernel, q_frames=Q, sps=sps, chained=chained,
                          permute_in=permute_in, pos0=pos0),
        grid=(n_seq, tiles_per_seq),
        in_specs=in_specs,
        out_specs=out_specs,
        out_shape=out_shape,
        scratch_shapes=scratch,
        compiler_params=pltpu.CompilerParams(
            dimension_semantics=("arbitrary", "arbitrary"),
            vmem_limit_bytes=VMEM_LIMIT_BYTES),
        name="mixer_chained" if chained else "mixer_streams",
    )(*args)


def _ffn_kernel(*refs, q_frames, final):
    Q = q_frames
    T = SUBLANES * Q
    n_blocks = T // ROWS
    x_ref, ada_ref, n2g_ref, wfi_ref, wfo_ref, fg_ref, o_ref = refs[:7]
    h_ref, act_ref = refs[7:9]

    n2g = n2g_ref[...]
    for i in range(n_blocks):
        rows = slice(i * ROWS, (i + 1) * ROWS)
        hrow = _rms(x_ref[rows, :]) * n2g * (1.0 + ada_ref[A_S2]) + ada_ref[A_SH2]
        h_ref[rows, :] = hrow.astype(BF16)

    h = h_ref[...]
    for f in range(D_FF // MXU_COLS):
        gate = _dot(h, wfi_ref[:, _colblk(f)])
        up = _dot(h, wfi_ref[:, _colblk(f, D_FF)])
        act_ref[:, _colblk(f)] = (_silu(gate) * up).astype(BF16)

    dst_ref = refs[9] if final else o_ref
    for n in range(N_COLBLK):
        cs = _colblk(n)
        y = _dot(act_ref[...], wfo_ref[:, cs])
        g2 = ada_ref[A_G2, :, cs]
        for i in range(n_blocks):
            rows = slice(i * ROWS, (i + 1) * ROWS)
            dst_ref[rows, cs] = x_ref[rows, cs] + g2 * y[rows, :]

    if final:
        xn_ref, ys_ref = refs[9:11]
        fg = fg_ref[...]
        def out_body(i):
            y = _rms(xn_ref[_blk(i), :]) * fg
            for c in range(N_CHUNKS):
                ys_ref[_at(c * T + i * ROWS, ROWS, ROWS), :] = y[:, c * LANES:(c + 1) * LANES]
        _loop(n_blocks, out_body, unroll=4)
        def unperm_body(i):
            r = i & (SUBLANES - 1)
            q0 = (i >> 3) * SUBLANES
            for c in range(N_CHUNKS):
                o_ref[_at(r * Q + q0, SUBLANES, SUBLANES), c * LANES:(c + 1) * LANES] = (
                    ys_ref[pl.ds(c * T + SUBLANES * q0 + r, SUBLANES, stride=SUBLANES), :])
        _loop(T // SUBLANES, unperm_body, unroll=2)


def _ffn_call(x, ada, n2g, wfi, wfo, fg, l, *, n_seq, tiles_per_seq, q_frames, final):
    Q = q_frames
    T = SUBLANES * Q
    n_tiles = n_seq * tiles_per_seq
    tile_idx = lambda b, j: (b * tiles_per_seq + j, 0)
    scratch = [
        pltpu.VMEM((T, D_MODEL), BF16),
        pltpu.VMEM((T, D_FF), BF16),
    ]
    if final:
        scratch += [pltpu.VMEM((T, D_MODEL), F32),
                    pltpu.VMEM((N_CHUNKS * T, LANES), F32)]
    return pl.pallas_call(
        functools.partial(_ffn_kernel, q_frames=Q, final=final),
        grid=(n_seq, tiles_per_seq),
        in_specs=[
            pl.BlockSpec((T, D_MODEL), tile_idx),
            pl.BlockSpec((None, 6, ROWS, D_MODEL), lambda b, j: (b, 0, 0, 0)),
            _layer_spec(n2g, l), _layer_spec(wfi, l), _layer_spec(wfo, l),
            pl.BlockSpec(fg.shape, lambda b, j: (0, 0), pipeline_mode=pl.Buffered(1)),
        ],
        out_specs=pl.BlockSpec((T, D_MODEL), tile_idx),
        out_shape=jax.ShapeDtypeStruct((n_tiles * T, D_MODEL), F32),
        scratch_shapes=scratch,
        compiler_params=pltpu.CompilerParams(
            dimension_semantics=("arbitrary", "arbitrary"),
            vmem_limit_bytes=VMEM_LIMIT_BYTES),
        name="ffn_final" if final else "ffn",
    )(x, ada, n2g, wfi, wfo, fg)


def _run_group(x, ada, hists, w_mixer, w_ffn, final_g, *, n_seq, tiles_per_seq,
               q_frames, sps, chained, pos0):
    states = []
    for l in range(DEPTH):
        x, sa, sb, sp = _mixer_call(
            x, ada[l], None if chained else hists[l], w_mixer, l,
            n_seq=n_seq, tiles_per_seq=tiles_per_seq, q_frames=q_frames, sps=sps,
            chained=chained, permute_in=(l == 0), pos0=pos0)
        x = _ffn_call(x, ada[l], *w_ffn, final_g, l, n_seq=n_seq,
                      tiles_per_seq=tiles_per_seq, q_frames=q_frames,
                      final=(l == DEPTH - 1))
        states.append((sa, sb, sp))
    return x, states


def kernel(x_prompt, x_sample, c_prompt, c_sample, cache_conv_a, cache_conv_b, cache_pool, w_ada, b_ada, norm1_g, w_in, w_conv_a, w_out_a, w_conv_b, b_conv_b, ln_b_g, ln_b_b, w_out_b, w_pool, pool_scale, w_o, norm2_g, w_ffn_in, w_ffn_out, final_g):
    bp, sp_len, _ = x_prompt.shape
    bs, ss_len, _ = x_sample.shape
    q_frames = 32
    tile = SUBLANES * q_frames
    assert q_frames >= K_B - 1 and sp_len % tile == 0 and ss_len % q_frames == 0
    sps_s = ss_len // q_frames
    seq_per_tile = SUBLANES // sps_s
    assert SUBLANES % sps_s == 0 and bs % seq_per_tile == 0
    n_sgrp = bs // seq_per_tile

    row = lambda v: v.reshape(DEPTH, 1, -1)
    w_mixer = [row(norm1_g), w_in.astype(BF16), w_conv_a, w_out_a.astype(BF16),
               w_conv_b, row(b_conv_b), row(ln_b_g), row(ln_b_b),
               w_out_b.astype(BF16), w_pool.astype(BF16), row(pool_scale),
               w_o.astype(BF16)]
    w_ffn = [row(norm2_g), w_ffn_in.astype(BF16), w_ffn_out.astype(BF16)]
    fg = final_g.reshape(1, D_MODEL)

    ada = _ada_call(jnp.concatenate([c_prompt, c_sample], axis=0), w_ada, b_ada)
    ada = ada.reshape(DEPTH, bp + bs, 6, D_MODEL)
    ada_p = jnp.broadcast_to(ada[:, :bp, :, None, :], (DEPTH, bp, 6, ROWS, D_MODEL))
    ada_s = ada[:, bp:].reshape(DEPTH, n_sgrp, seq_per_tile, 6, D_MODEL)
    ada_s = jnp.repeat(jnp.transpose(ada_s, (0, 1, 3, 2, 4)), sps_s, axis=3)
    ada_s = jnp.tile(ada_s, (1, 1, 1, ROWS // SUBLANES, 1))

    def frame_major(cache):
        n = cache.shape[2]
        c = cache.reshape(DEPTH, n_sgrp, seq_per_tile, n, D_MODEL)
        return jnp.repeat(jnp.transpose(c, (0, 1, 3, 2, 4)), sps_s, axis=3)
    hists = list(zip(frame_major(cache_conv_a), frame_major(cache_conv_b),
                     frame_major(cache_pool)))

    yp, st_p = _run_group(
        x_prompt.reshape(bp * sp_len, D_MODEL), ada_p, None, w_mixer, w_ffn, fg,
        n_seq=bp, tiles_per_seq=sp_len // tile, q_frames=q_frames,
        sps=SUBLANES, chained=True, pos0=0)
    ys, st_s = _run_group(
        x_sample.reshape(bs * ss_len, D_MODEL), ada_s, hists, w_mixer, w_ffn, fg,
        n_seq=n_sgrp, tiles_per_seq=1, q_frames=q_frames, sps=sps_s, chained=False,
        pos0=PAST_LEN)

    def seq_major(s):
        return s.reshape(bs, s.shape[2], D_MODEL)

    return (yp.reshape(bp, sp_len, D_MODEL), ys.reshape(bs, ss_len, D_MODEL),
            jnp.stack([s[0] for s in st_p]), jnp.stack([s[1] for s in st_p]),
            jnp.stack([s[2] for s in st_p]),
            jnp.stack([seq_major(s[0]) for s in st_s]),
            jnp.stack([seq_major(s[1]) for s in st_s]),
            jnp.stack([seq_major(s[2]) for s in st_s]))
```

```python
import functools

import jax
import jax.numpy as jnp
from jax import lax
from jax.experimental import pallas as pl
from jax.experimental.pallas import tpu as pltpu

F32 = jnp.float32
BF16 = jnp.bfloat16

D_MODEL = 1024
DEPTH = 2
EPS = 1e-6
K_A = 3
K_B = 31
POOL_WINDOWS = (2, 4, 8, 16)
N_POOL = len(POOL_WINDOWS)
POOL_GROUP = D_MODEL // N_POOL
POOL_PAD = max(POOL_WINDOWS) - 1
N_IN = 9 * D_MODEL
D_FF = 2816
PAST_LEN = 1024

SUBLANES = 8
LANES = 128
ROWS = 16
N_CHUNKS = D_MODEL // LANES
MXU_COLS = 256
N_COLBLK = D_MODEL // MXU_COLS
CONV_CHAIN = 16
VMEM_LIMIT_BYTES = 60 * 1024 * 1024

C_BG, C_CG, C_HA, C_GA, C_GB, C_PIN, C_G0, C_G1, C_G2 = (i * D_MODEL for i in range(9))

A_SH1, A_S1, A_G1, A_SH2, A_S2, A_G2 = range(6)


def _sigmoid(x):
    return 0.5 * jnp.tanh(0.5 * x) + 0.5


def _silu(x):
    hx = 0.5 * x
    return hx + hx * jnp.tanh(hx)


def _dot(a, b):
    return jnp.dot(a, b, preferred_element_type=F32)


def _blk(i, n=ROWS):
    return pl.ds(pl.multiple_of(i * n, n), n)


def _at(start, n, align):
    return pl.ds(pl.multiple_of(start, align), n)


def _colblk(n, c0=0):
    return slice(c0 + n * MXU_COLS, c0 + (n + 1) * MXU_COLS)


def _loop(n, body, unroll=1):
    def step(i, carry):
        body(i)
        return carry
    lax.fori_loop(0, n, step, 0, unroll=unroll)


def _rms(x):
    return x * lax.rsqrt(jnp.mean(x * x, axis=-1, keepdims=True) + EPS)


def _ada_kernel(c_ref, w_ref, b_ref, o_ref):
    sc = _silu(c_ref[...]).astype(BF16)
    o_ref[...] = _dot(sc, w_ref[...].astype(BF16)) + b_ref[...]


def _ada_call(c_all, w_ada, b_ada):
    n = c_all.shape[0]
    tn = 1536
    return pl.pallas_call(
        _ada_kernel,
        grid=(DEPTH, 6 * D_MODEL // tn),
        in_specs=[
            pl.BlockSpec((n, D_MODEL), lambda l, j: (0, 0)),
            pl.BlockSpec((None, D_MODEL, tn), lambda l, j: (l, 0, j)),
            pl.BlockSpec((None, 1, tn), lambda l, j: (l, 0, j)),
        ],
        out_specs=pl.BlockSpec((None, n, tn), lambda l, j: (l, 0, j)),
        out_shape=jax.ShapeDtypeStruct((DEPTH, n, 6 * D_MODEL), F32),
        compiler_params=pltpu.CompilerParams(
            dimension_semantics=("arbitrary", "arbitrary"),
            vmem_limit_bytes=VMEM_LIMIT_BYTES),
        name="ada",
    )(c_all, w_ada, b_ada.reshape(DEPTH, 1, 6 * D_MODEL))


def _mixer_kernel(*refs, q_frames, sps, chained, permute_in, pos0):
    Q = q_frames
    T = SUBLANES * Q
    refs = list(refs)
    n_x = N_CHUNKS if permute_in else 1
    x_refs = refs[:n_x]
    ada_ref = refs[n_x]
    refs = refs[n_x + 1:]
    if not chained:
        hist_a_ref, hist_b_ref, hist_p_ref = refs[:3]
        refs = refs[3:]
    else:
        hist_a_ref = hist_b_ref = hist_p_ref = None
    (n1g_ref, win_ref, wca_ref, woa_ref, wcb_ref, bcb_ref, lng_ref, lnb_ref,
     wob_ref, wpool_ref, pscale_ref, wo_ref) = refs[:12]
    refs = refs[12:]
    xo_ref, sa_ref, sb_ref, sp_ref = refs[:4]
    refs = refs[4:]
    h_ref, sg_ref, m_ref, cb_ref, eu_ref, ev_ref, ep_ref, a_ref, b_ref = refs[:9]
    refs = refs[9:]
    if chained:
        cu_ref, cv_ref, cp_ref = refs[:3]
        refs = refs[3:]
    else:
        cu_ref = cv_ref = cp_ref = None
    xs_ref = refs[0] if permute_in else x_refs[0]

    j = pl.program_id(1)
    n_blocks = T // ROWS

    if permute_in:
        def perm_body(q):
            for c in range(N_CHUNKS):
                xs_ref[_blk(q, SUBLANES), c * LANES:(c + 1) * LANES] = (
                    x_refs[c][pl.ds(q, SUBLANES, stride=Q), :])
        _loop(Q, perm_body, unroll=2)

    if chained:
        @pl.when(j == 0)
        def _():
            cu_ref[...] = jnp.zeros_like(cu_ref)
            cv_ref[...] = jnp.zeros_like(cv_ref)
            cp_ref[...] = jnp.zeros_like(cp_ref)

    stream = lax.broadcasted_iota(jnp.int32, (SUBLANES, MXU_COLS), 0)
    first = (stream & (sps - 1)) == 0

    def fill_halo(e_ref, carry_ref, hist_ref, n_halo, cs):
        for g in range(n_halo):
            dst = slice(SUBLANES * g, SUBLANES * (g + 1))
            tail = e_ref[SUBLANES * (Q + g):SUBLANES * (Q + g + 1), cs]
            prev = pltpu.roll(tail, 1, axis=0)
            hist = carry_ref[dst, cs] if chained else hist_ref[g, :, cs]
            e_ref[dst, cs] = jnp.where(first, hist, prev)
            if chained:
                carry_ref[dst, cs] = prev

    n1g = n1g_ref[...]
    for i in range(n_blocks):
        rows = slice(i * ROWS, (i + 1) * ROWS)
        hrow = _rms(xs_ref[rows, :]) * n1g * (1.0 + ada_ref[A_S1]) + ada_ref[A_SH1]
        h_ref[rows, :] = hrow.astype(BF16)

    if pos0 < POOL_PAD:
        rr = lax.broadcasted_iota(jnp.int32, (T, 1), 0)
        frame = j * T + (rr & (SUBLANES - 1)) * Q + (rr >> 3)
        avail = pos0 + frame + 1

    h = h_ref[...]
    pscale = pscale_ref[...]
    for n in range(N_COLBLK):
        cs = _colblk(n)
        proj = lambda c0: _dot(h, win_ref[:, _colblk(n, c0)])

        eu_ref[SUBLANES * (K_A - 1):, cs] = proj(C_CG) * proj(C_HA)
        fill_halo(eu_ref, cu_ref, hist_a_ref, K_A - 1, cs)
        acc = None
        for k in range(K_A):
            t = eu_ref[SUBLANES * k:SUBLANES * k + T, cs] * wca_ref[k:k + 1, cs]
            acc = t if acc is None else acc + t
        a_ref[:, cs] = (proj(C_BG) * acc).astype(BF16)

        ev_ref[SUBLANES * (K_B - 1):, cs] = proj(C_GA) * _sigmoid(proj(C_GB))
        fill_halo(ev_ref, cv_ref, hist_b_ref, K_B - 1, cs)
        for cc in range(MXU_COLS // LANES):
            col = slice(cs.start + cc * LANES, cs.start + (cc + 1) * LANES)
            wk = [jnp.broadcast_to(wcb_ref[k:k + 1, col].astype(BF16), (ROWS, LANES))
                  for k in range(K_B)]
            n_win = 2 * (Q // 2 - 1) + K_B
            win = [ev_ref[SUBLANES * g:SUBLANES * g + ROWS, col].astype(BF16)
                   for g in range(n_win)]
            bias = jnp.broadcast_to(bcb_ref[:, col], (ROWS, LANES))
            for i in range(Q // 2):
                acc = bias
                for k0 in range(0, K_B, CONV_CHAIN):
                    run = None
                    for k in range(k0, min(k0 + CONV_CHAIN, K_B)):
                        t = win[2 * i + k] * wk[k]
                        run = t if run is None else run + t
                    acc = acc + run.astype(F32)
                cb_ref[i * ROWS:(i + 1) * ROWS, col] = acc

        for gi, c0 in enumerate((C_G0, C_G1, C_G2)):
            sg_ref[gi, :, cs] = _sigmoid(proj(c0))

        ep_ref[SUBLANES * POOL_PAD:, cs] = proj(C_PIN)
        fill_halo(ep_ref, cp_ref, hist_p_ref, POOL_PAD, cs)
        w = POOL_WINDOWS[n]
        cur = ep_ref[SUBLANES * POOL_PAD:, cs]
        acc = cur
        for s in range(1, w):
            acc = acc + ep_ref[SUBLANES * (POOL_PAD - s):SUBLANES * (POOL_PAD - s) + T, cs]
        if pos0 < POOL_PAD:
            inv = 1.0 / jnp.minimum(avail, w).astype(F32)
        else:
            inv = 1.0 / w
        pooled = (acc * inv - cur).astype(BF16)
        y_c = _dot(pooled, wpool_ref[n])
        m_ref[:, cs] = sg_ref[2, :, cs] * (y_c * pscale[:, cs])

    for n in range(N_COLBLK):
        cs = _colblk(n)
        m_ref[:, cs] += sg_ref[0, :, cs] * _dot(a_ref[...], woa_ref[:, cs])

    lng = lng_ref[...]
    lnb = lnb_ref[...]
    for i in range(n_blocks):
        rows = slice(i * ROWS, (i + 1) * ROWS)
        cb = cb_ref[rows, :]
        mu = jnp.mean(cb, axis=-1, keepdims=True)
        d = cb - mu
        var = jnp.mean(d * d, axis=-1, keepdims=True)
        y = d * lax.rsqrt(var + EPS) * lng + lnb
        b_ref[rows, :] = _silu(y).astype(BF16)

    for n in range(N_COLBLK):
        cs = _colblk(n)
        m = m_ref[:, cs] + sg_ref[1, :, cs] * _dot(b_ref[...], wob_ref[:, cs])
        h_ref[:, cs] = m.astype(BF16)

    for n in range(N_COLBLK):
        cs = _colblk(n)
        out = _dot(h_ref[...], wo_ref[:, cs])
        g1 = ada_ref[A_G1, :, cs]
        for i in range(n_blocks):
            rows = slice(i * ROWS, (i + 1) * ROWS)
            xo_ref[rows, cs] = xs_ref[rows, cs] + g1 * out[rows, :]

    if chained:
        @pl.when(j == pl.num_programs(1) - 1)
        def _():
            for carry_ref, s_ref, n_halo in ((cu_ref, sa_ref, K_A - 1),
                                             (cv_ref, sb_ref, K_B - 1),
                                             (cp_ref, sp_ref, POOL_PAD)):
                for g in range(n_halo):
                    s_ref[g:g + 1, :] = carry_ref[SUBLANES * g:SUBLANES * g + 1, :]
    else:
        for e_ref, s_ref, n_halo in ((eu_ref, sa_ref, K_A - 1),
                                     (ev_ref, sb_ref, K_B - 1),
                                     (ep_ref, sp_ref, POOL_PAD)):
            for s in range(SUBLANES // sps):
                r = sps * s + sps - 1
                for g in range(n_halo):
                    row = SUBLANES * (Q + g) + r
                    s_ref[s, g:g + 1, :] = e_ref[row:row + 1, :]


def _layer_spec(arr, l):
    nd = arr.ndim - 1
    return pl.BlockSpec((None,) + arr.shape[1:], lambda *_: (l,) + (0,) * nd,
                        pipeline_mode=pl.Buffered(1))


def _mixer_call(x, ada, hists, w, l, *, n_seq, tiles_per_seq, q_frames, sps, chained,
                permute_in, pos0):
    Q = q_frames
    T = SUBLANES * Q
    n_tiles = n_seq * tiles_per_seq
    tile_idx = lambda b, j: (b * tiles_per_seq + j, 0)
    if permute_in:
        in_specs = [pl.BlockSpec((T, LANES), lambda b, j, c=c: (b * tiles_per_seq + j, c))
                    for c in range(N_CHUNKS)]
        args = [x] * N_CHUNKS
    else:
        in_specs = [pl.BlockSpec((T, D_MODEL), tile_idx)]
        args = [x]
    in_specs.append(pl.BlockSpec((None, 6, ROWS, D_MODEL), lambda b, j: (b, 0, 0, 0)))
    args.append(ada)
    if not chained:
        for hst in hists:
            in_specs.append(pl.BlockSpec((None,) + hst.shape[1:], lambda b, j: (b, 0, 0, 0)))
            args.append(hst)
    for arr in w:
        in_specs.append(_layer_spec(arr, l))
        args.append(arr)

    if chained:
        st_shape = lambda n: jax.ShapeDtypeStruct((n_seq, n, D_MODEL), F32)
        st_spec = lambda n: pl.BlockSpec((None, n, D_MODEL), lambda b, j: (b, 0, 0))
    else:
        spt = SUBLANES // sps
        st_shape = lambda n: jax.ShapeDtypeStruct((n_seq, spt, n, D_MODEL), F32)
        st_spec = lambda n: pl.BlockSpec((None, spt, n, D_MODEL), lambda b, j: (b, 0, 0, 0))
    halos = (K_A - 1, K_B - 1, POOL_PAD)
    out_shape = [jax.ShapeDtypeStruct((n_tiles * T, D_MODEL), F32)] + [st_shape(n) for n in halos]
    out_specs = [pl.BlockSpec((T, D_MODEL), tile_idx)] + [st_spec(n) for n in halos]

    scratch = [
        pltpu.VMEM((T, D_MODEL), BF16),
        pltpu.VMEM((3, T, D_MODEL), F32),
        pltpu.VMEM((T, D_MODEL), F32),
        pltpu.VMEM((T, D_MODEL), F32),
    ] + [pltpu.VMEM((SUBLANES * (n + Q), D_MODEL), F32) for n in halos] + [
        pltpu.VMEM((T, D_MODEL), BF16),
        pltpu.VMEM((T, D_MODEL), BF16),
    ]
    if chained:
        scratch += [pltpu.VMEM((SUBLANES * n, D_MODEL), F32) for n in halos]
    if permute_in:
        scratch.append(pltpu.VMEM((T, D_MODEL), F32))

    return pl.pallas_call(
        functools.partial(_mixer_kernel, q_frames=Q, sps=sps, chained=chained,
                          permute_in=permute_in, pos0=pos0),
        grid=(n_seq, tiles_per_seq),
        in_specs=in_specs,
        out_specs=out_specs,
        out_shape=out_shape,
        scratch_shapes=scratch,
        compiler_params=pltpu.CompilerParams(
            dimension_semantics=("arbitrary", "arbitrary"),
            vmem_limit_bytes=VMEM_LIMIT_BYTES),
        name="mixer_chained" if chained else "mixer_streams",
    )(*args)


def _ffn_kernel(*refs, q_frames, final):
    Q = q_frames
    T = SUBLANES * Q
    n_blocks = T // ROWS
    x_ref, ada_ref, n2g_ref, wfi_ref, wfo_ref, fg_ref, o_ref = refs[:7]
    h_ref, act_ref = refs[7:9]

    n2g = n2g_ref[...]
    for i in range(n_blocks):
        rows = slice(i * ROWS, (i + 1) * ROWS)
        hrow = _rms(x_ref[rows, :]) * n2g * (1.0 + ada_ref[A_S2]) + ada_ref[A_SH2]
        h_ref[rows, :] = hrow.astype(BF16)

    h = h_ref[...]
    for f in range(D_FF // MXU_COLS):
        gate = _dot(h, wfi_ref[:, _colblk(f)])
        up = _dot(h, wfi_ref[:, _colblk(f, D_FF)])
        act_ref[:, _colblk(f)] = (_silu(gate) * up).astype(BF16)

    dst_ref = refs[9] if final else o_ref
    for n in range(N_COLBLK):
        cs = _colblk(n)
        y = _dot(act_ref[...], wfo_ref[:, cs])
        g2 = ada_ref[A_G2, :, cs]
        for i in range(n_blocks):
            rows = slice(i * ROWS, (i + 1) * ROWS)
            dst_ref[rows, cs] = x_ref[rows, cs] + g2 * y[rows, :]

    if final:
        xn_ref, ys_ref = refs[9:11]
        fg = fg_ref[...]
        def out_body(i):
            y = _rms(xn_ref[_blk(i), :]) * fg
            for c in range(N_CHUNKS):
                ys_ref[_at(c * T + i * ROWS, ROWS, ROWS), :] = y[:, c * LANES:(c + 1) * LANES]
        _loop(n_blocks, out_body, unroll=4)
        def unperm_body(i):
            r = i & (SUBLANES - 1)
            q0 = (i >> 3) * SUBLANES
            for c in range(N_CHUNKS):
                o_ref[_at(r * Q + q0, SUBLANES, SUBLANES), c * LANES:(c + 1) * LANES] = (
                    ys_ref[pl.ds(c * T + SUBLANES * q0 + r, SUBLANES, stride=SUBLANES), :])
        _loop(T // SUBLANES, unperm_body, unroll=2)


def _ffn_call(x, ada, n2g, wfi, wfo, fg, l, *, n_seq, tiles_per_seq, q_frames, final):
    Q = q_frames
    T = SUBLANES * Q
    n_tiles = n_seq * tiles_per_seq
    tile_idx = lambda b, j: (b * tiles_per_seq + j, 0)
    scratch = [
        pltpu.VMEM((T, D_MODEL), BF16),
        pltpu.VMEM((T, D_FF), BF16),
    ]
    if final:
        scratch += [pltpu.VMEM((T, D_MODEL), F32),
                    pltpu.VMEM((N_CHUNKS * T, LANES), F32)]
    return pl.pallas_call(
        functools.partial(_ffn_kernel, q_frames=Q, final=final),
        grid=(n_seq, tiles_per_seq),
        in_specs=[
            pl.BlockSpec((T, D_MODEL), tile_idx),
            pl.BlockSpec((None, 6, ROWS, D_MODEL), lambda b, j: (b, 0, 0, 0)),
            _layer_spec(n2g, l), _layer_spec(wfi, l), _layer_spec(wfo, l),
            pl.BlockSpec(fg.shape, lambda b, j: (0, 0), pipeline_mode=pl.Buffered(1)),
        ],
        out_specs=pl.BlockSpec((T, D_MODEL), tile_idx),
        out_shape=jax.ShapeDtypeStruct((n_tiles * T, D_MODEL), F32),
        scratch_shapes=scratch,
        compiler_params=pltpu.CompilerParams(
            dimension_semantics=("arbitrary", "arbitrary"),
            vmem_limit_bytes=VMEM_LIMIT_BYTES),
        name="ffn_final" if final else "ffn",
    )(x, ada, n2g, wfi, wfo, fg)


def _run_group(x, ada, hists, w_mixer, w_ffn, final_g, *, n_seq, tiles_per_seq,
               q_frames, sps, chained, pos0):
    states = []
    for l in range(DEPTH):
        x, sa, sb, sp = _mixer_call(
            x, ada[l], None if chained else hists[l], w_mixer, l,
            n_seq=n_seq, tiles_per_seq=tiles_per_seq, q_frames=q_frames, sps=sps,
            chained=chained, permute_in=(l == 0), pos0=pos0)
        x = _ffn_call(x, ada[l], *w_ffn, final_g, l, n_seq=n_seq,
                      tiles_per_seq=tiles_per_seq, q_frames=q_frames,
                      final=(l == DEPTH - 1))
        states.append((sa, sb, sp))
    return x, states


def kernel(x_prompt, x_sample, c_prompt, c_sample, cache_conv_a, cache_conv_b, cache_pool, w_ada, b_ada, norm1_g, w_in, w_conv_a, w_out_a, w_conv_b, b_conv_b, ln_b_g, ln_b_b, w_out_b, w_pool, pool_scale, w_o, norm2_g, w_ffn_in, w_ffn_out, final_g):
    bp, sp_len, _ = x_prompt.shape
    bs, ss_len, _ = x_sample.shape
    q_frames = 64
    tile = SUBLANES * q_frames
    assert q_frames >= K_B - 1 and sp_len % tile == 0 and ss_len % q_frames == 0
    sps_s = ss_len // q_frames
    seq_per_tile = SUBLANES // sps_s
    assert SUBLANES % sps_s == 0 and bs % seq_per_tile == 0
    n_sgrp = bs // seq_per_tile

    row = lambda v: v.reshape(DEPTH, 1, -1)
    w_mixer = [row(norm1_g), w_in.astype(BF16), w_conv_a, w_out_a.astype(BF16),
               w_conv_b, row(b_conv_b), row(ln_b_g), row(ln_b_b),
               w_out_b.astype(BF16), w_pool.astype(BF16), row(pool_scale),
               w_o.astype(BF16)]
    w_ffn = [row(norm2_g), w_ffn_in.astype(BF16), w_ffn_out.astype(BF16)]
    fg = final_g.reshape(1, D_MODEL)

    ada = _ada_call(jnp.concatenate([c_prompt, c_sample], axis=0), w_ada, b_ada)
    ada = ada.reshape(DEPTH, bp + bs, 6, D_MODEL)
    ada_p = jnp.broadcast_to(ada[:, :bp, :, None, :], (DEPTH, bp, 6, ROWS, D_MODEL))
    ada_s = ada[:, bp:].reshape(DEPTH, n_sgrp, seq_per_tile, 6, D_MODEL)
    ada_s = jnp.repeat(jnp.transpose(ada_s, (0, 1, 3, 2, 4)), sps_s, axis=3)
    ada_s = jnp.tile(ada_s, (1, 1, 1, ROWS // SUBLANES, 1))

    def frame_major(cache):
        n = cache.shape[2]
        c = cache.reshape(DEPTH, n_sgrp, seq_per_tile, n, D_MODEL)
        return jnp.repeat(jnp.transpose(c, (0, 1, 3, 2, 4)), sps_s, axis=3)
    hists = list(zip(frame_major(cache_conv_a), frame_major(cache_conv_b),
                     frame_major(cache_pool)))

    yp, st_p = _run_group(
        x_prompt.reshape(bp * sp_len, D_MODEL), ada_p, None, w_mixer, w_ffn, fg,
        n_seq=bp, tiles_per_seq=sp_len // tile, q_frames=q_frames,
        sps=SUBLANES, chained=True, pos0=0)
    ys, st_s = _run_group(
        x_sample.reshape(bs * ss_len, D_MODEL), ada_s, hists, w_mixer, w_ffn, fg,
        n_seq=n_sgrp, tiles_per_seq=1, q_frames=q_frames, sps=sps_s, chained=False,
        pos0=PAST_LEN)

    def seq_major(s):
        return s.reshape(bs, s.shape[2], D_MODEL)

    return (yp.reshape(bp, sp_len, D_MODEL), ys.reshape(bs, ss_len, D_MODEL),
            jnp.stack([s[0] for s in st_p]), jnp.stack([s[1] for s in st_p]),
            jnp.stack([s[2] for s in st_p]),
            jnp.stack([seq_major(s[0]) for s in st_s]),
            jnp.stack([seq_major(s[1]) for s in st_s]),
            jnp.stack([seq_major(s[2]) for s in st_s]))
```

```python
import functools

import jax
import jax.numpy as jnp
from jax import lax
from jax.experimental import pallas as pl
from jax.experimental.pallas import tpu as pltpu

F32 = jnp.float32
BF16 = jnp.bfloat16

D_MODEL = 1024
DEPTH = 2
EPS = 1e-6
K_A = 3
K_B = 31
POOL_WINDOWS = (2, 4, 8, 16)
N_POOL = len(POOL_WINDOWS)
POOL_GROUP = D_MODEL // N_POOL
POOL_PAD = max(POOL_WINDOWS) - 1
N_IN = 9 * D_MODEL
D_FF = 2816
PAST_LEN = 1024

SUBLANES = 8
LANES = 128
ROWS = 16
N_CHUNKS = D_MODEL // LANES
MXU_COLS = 256
N_COLBLK = D_MODEL // MXU_COLS
CONV_CHAIN = 16
VMEM_LIMIT_BYTES = 60 * 1024 * 1024

C_BG, C_CG, C_HA, C_GA, C_GB, C_PIN, C_G0, C_G1, C_G2 = (i * D_MODEL for i in range(9))

A_SH1, A_S1, A_G1, A_SH2, A_S2, A_G2 = range(6)


def _sigmoid(x):
    return 0.5 * jnp.tanh(0.5 * x) + 0.5


def _silu(x):
    hx = 0.5 * x
    return hx + hx * jnp.tanh(hx)


def _dot(a, b):
    return jnp.dot(a, b, preferred_element_type=F32)


def _colblk(n, c0=0):
    return slice(c0 + n * MXU_COLS, c0 + (n + 1) * MXU_COLS)


def _rms(x):
    return x * lax.rsqrt(jnp.mean(x * x, axis=-1, keepdims=True) + EPS)


def _ada_kernel(c_ref, w_ref, b_ref, o_ref):
    sc = _silu(c_ref[...]).astype(BF16)
    o_ref[...] = _dot(sc, w_ref[...].astype(BF16)) + b_ref[...]


def _ada_call(c_all, w_ada, b_ada):
    n = c_all.shape[0]
    tn = 1536
    return pl.pallas_call(
        _ada_kernel,
        grid=(DEPTH, 6 * D_MODEL // tn),
        in_specs=[
            pl.BlockSpec((n, D_MODEL), lambda l, j: (0, 0)),
            pl.BlockSpec((None, D_MODEL, tn), lambda l, j: (l, 0, j)),
            pl.BlockSpec((None, 1, tn), lambda l, j: (l, 0, j)),
        ],
        out_specs=pl.BlockSpec((None, n, tn), lambda l, j: (l, 0, j)),
        out_shape=jax.ShapeDtypeStruct((DEPTH, n, 6 * D_MODEL), F32),
        compiler_params=pltpu.CompilerParams(
            dimension_semantics=("arbitrary", "arbitrary"),
            vmem_limit_bytes=VMEM_LIMIT_BYTES),
        name="ada",
    )(c_all, w_ada, b_ada.reshape(DEPTH, 1, 6 * D_MODEL))


def _mixer_kernel(*refs, q_frames, sps, chained, permute_in, pos0):
    Q = q_frames
    T = SUBLANES * Q
    refs = list(refs)
    x_ref, ada_ref = refs[:2]
    refs = refs[2:]
    if not chained:
        hist_a_ref, hist_b_ref, hist_p_ref = refs[:3]
        refs = refs[3:]
    else:
        hist_a_ref = hist_b_ref = hist_p_ref = None
    (n1g_ref, win_ref, wca_ref, woa_ref, wcb_ref, bcb_ref, lng_ref, lnb_ref,
     wob_ref, wpool_ref, pscale_ref, wo_ref) = refs[:12]
    refs = refs[12:]
    xo_ref, sa_ref, sb_ref, sp_ref = refs[:4]
    refs = refs[4:]
    h_ref, sg_ref, m_ref, cb_ref, eu_ref, ev_ref, ep_ref, a_ref, b_ref = refs[:9]
    refs = refs[9:]
    if chained:
        cu_ref, cv_ref, cp_ref = refs[:3]
        refs = refs[3:]
    else:
        cu_ref = cv_ref = cp_ref = None
    xs_ref = refs[0] if permute_in else x_ref

    j = pl.program_id(1)
    n_blocks = T // ROWS

    if permute_in:
        for c in range(N_CHUNKS):
            col = slice(c * LANES, (c + 1) * LANES)
            by_stream = x_ref[:, col].reshape(SUBLANES, Q, LANES)
            xs_ref[:, col] = jnp.swapaxes(by_stream, 0, 1).reshape(T, LANES)

    if chained:
        @pl.when(j == 0)
        def _():
            cu_ref[...] = jnp.zeros_like(cu_ref)
            cv_ref[...] = jnp.zeros_like(cv_ref)
            cp_ref[...] = jnp.zeros_like(cp_ref)

    stream = lax.broadcasted_iota(jnp.int32, (SUBLANES, MXU_COLS), 0)
    first = (stream & (sps - 1)) == 0

    def fill_halo(e_ref, carry_ref, hist_ref, n_halo, cs):
        for g in range(n_halo):
            dst = slice(SUBLANES * g, SUBLANES * (g + 1))
            tail = e_ref[SUBLANES * (Q + g):SUBLANES * (Q + g + 1), cs]
            prev = pltpu.roll(tail, 1, axis=0)
            hist = carry_ref[dst, cs] if chained else hist_ref[g, :, cs]
            e_ref[dst, cs] = jnp.where(first, hist, prev)
            if chained:
                carry_ref[dst, cs] = prev

    n1g = n1g_ref[...]
    for i in range(n_blocks):
        rows = slice(i * ROWS, (i + 1) * ROWS)
        hrow = _rms(xs_ref[rows, :]) * n1g * (1.0 + ada_ref[A_S1]) + ada_ref[A_SH1]
        h_ref[rows, :] = hrow.astype(BF16)

    if pos0 < POOL_PAD:
        rr = lax.broadcasted_iota(jnp.int32, (T, 1), 0)
        frame = j * T + (rr & (SUBLANES - 1)) * Q + (rr >> 3)
        avail = pos0 + frame + 1

    h = h_ref[...]
    pscale = pscale_ref[...]
    for n in range(N_COLBLK):
        cs = _colblk(n)
        proj = lambda c0: _dot(h, win_ref[:, _colblk(n, c0)])

        eu_ref[SUBLANES * (K_A - 1):, cs] = proj(C_CG) * proj(C_HA)
        fill_halo(eu_ref, cu_ref, hist_a_ref, K_A - 1, cs)
        acc = None
        for k in range(K_A):
            t = eu_ref[SUBLANES * k:SUBLANES * k + T, cs] * wca_ref[k:k + 1, cs]
            acc = t if acc is None else acc + t
        a_ref[:, cs] = (proj(C_BG) * acc).astype(BF16)

        ev_ref[SUBLANES * (K_B - 1):, cs] = proj(C_GA) * _sigmoid(proj(C_GB))
        fill_halo(ev_ref, cv_ref, hist_b_ref, K_B - 1, cs)
        for cc in range(MXU_COLS // LANES):
            col = slice(cs.start + cc * LANES, cs.start + (cc + 1) * LANES)
            wk = [jnp.broadcast_to(wcb_ref[k:k + 1, col].astype(BF16), (ROWS, LANES))
                  for k in range(K_B)]
            n_win = 2 * (Q // 2 - 1) + K_B
            win = [ev_ref[SUBLANES * g:SUBLANES * g + ROWS, col].astype(BF16)
                   for g in range(n_win)]
            bias = jnp.broadcast_to(bcb_ref[:, col], (ROWS, LANES))
            for i in range(Q // 2):
                acc = bias
                for k0 in range(0, K_B, CONV_CHAIN):
                    run = None
                    for k in range(k0, min(k0 + CONV_CHAIN, K_B)):
                        t = win[2 * i + k] * wk[k]
                        run = t if run is None else run + t
                    acc = acc + run.astype(F32)
                cb_ref[i * ROWS:(i + 1) * ROWS, col] = acc

        for gi, c0 in enumerate((C_G0, C_G1, C_G2)):
            sg_ref[gi, :, cs] = _sigmoid(proj(c0))

        ep_ref[SUBLANES * POOL_PAD:, cs] = proj(C_PIN)
        fill_halo(ep_ref, cp_ref, hist_p_ref, POOL_PAD, cs)
        w = POOL_WINDOWS[n]
        cur = ep_ref[SUBLANES * POOL_PAD:, cs]
        acc = cur
        for s in range(1, w):
            acc = acc + ep_ref[SUBLANES * (POOL_PAD - s):SUBLANES * (POOL_PAD - s) + T, cs]
        if pos0 < POOL_PAD:
            inv = 1.0 / jnp.minimum(avail, w).astype(F32)
        else:
            inv = 1.0 / w
        pooled = (acc * inv - cur).astype(BF16)
        y_c = _dot(pooled, wpool_ref[n])
        m_ref[:, cs] = sg_ref[2, :, cs] * (y_c * pscale[:, cs])

    for n in range(N_COLBLK):
        cs = _colblk(n)
        m_ref[:, cs] += sg_ref[0, :, cs] * _dot(a_ref[...], woa_ref[:, cs])

    lng = lng_ref[...]
    lnb = lnb_ref[...]
    for i in range(n_blocks):
        rows = slice(i * ROWS, (i + 1) * ROWS)
        cb = cb_ref[rows, :]
        mu = jnp.mean(cb, axis=-1, keepdims=True)
        d = cb - mu
        var = jnp.mean(d * d, axis=-1, keepdims=True)
        y = d * lax.rsqrt(var + EPS) * lng + lnb
        b_ref[rows, :] = _silu(y).astype(BF16)

    for n in range(N_COLBLK):
        cs = _colblk(n)
        m = m_ref[:, cs] + sg_ref[1, :, cs] * _dot(b_ref[...], wob_ref[:, cs])
        h_ref[:, cs] = m.astype(BF16)

    for n in range(N_COLBLK):
        cs = _colblk(n)
        out = _dot(h_ref[...], wo_ref[:, cs])
        g1 = ada_ref[A_G1, :, cs]
        for i in range(n_blocks):
            rows = slice(i * ROWS, (i + 1) * ROWS)
            xo_ref[rows, cs] = xs_ref[rows, cs] + g1 * out[rows, :]

    if chained:
        @pl.when(j == pl.num_programs(1) - 1)
        def _():
            for carry_ref, s_ref, n_halo in ((cu_ref, sa_ref, K_A - 1),
                                             (cv_ref, sb_ref, K_B - 1),
                                             (cp_ref, sp_ref, POOL_PAD)):
                for g in range(n_halo):
                    s_ref[g:g + 1, :] = carry_ref[SUBLANES * g:SUBLANES * g + 1, :]
    else:
        for e_ref, s_ref, n_halo in ((eu_ref, sa_ref, K_A - 1),
                                     (ev_ref, sb_ref, K_B - 1),
                                     (ep_ref, sp_ref, POOL_PAD)):
            for s in range(SUBLANES // sps):
                r = sps * s + sps - 1
                for g in range(n_halo):
                    row = SUBLANES * (Q + g) + r
                    s_ref[s, g:g + 1, :] = e_ref[row:row + 1, :]


def _layer_spec(arr, l):
    nd = arr.ndim - 1
    return pl.BlockSpec((None,) + arr.shape[1:], lambda *_: (l,) + (0,) * nd,
                        pipeline_mode=pl.Buffered(1))


def _mixer_call(x, ada, hists, w, l, *, n_seq, tiles_per_seq, q_frames, sps, chained,
                permute_in, pos0):
    Q = q_frames
    T = SUBLANES * Q
    n_tiles = n_seq * tiles_per_seq
    tile_idx = lambda b, j: (b * tiles_per_seq + j, 0)
    in_specs = [pl.BlockSpec((T, D_MODEL), tile_idx)]
    args = [x]
    in_specs.append(pl.BlockSpec((None, 6, ROWS, D_MODEL), lambda b, j: (b, 0, 0, 0)))
    args.append(ada)
    if not chained:
        for hst in hists:
            in_specs.append(pl.BlockSpec((None,) + hst.shape[1:], lambda b, j: (b, 0, 0, 0)))
            args.append(hst)
    for arr in w:
        in_specs.append(_layer_spec(arr, l))
        args.append(arr)

    if chained:
        st_shape = lambda n: jax.ShapeDtypeStruct((n_seq, n, D_MODEL), F32)
        st_spec = lambda n: pl.BlockSpec((None, n, D_MODEL), lambda b, j: (b, 0, 0))
    else:
        spt = SUBLANES // sps
        st_shape = lambda n: jax.ShapeDtypeStruct((n_seq, spt, n, D_MODEL), F32)
        st_spec = lambda n: pl.BlockSpec((None, spt, n, D_MODEL), lambda b, j: (b, 0, 0, 0))
    halos = (K_A - 1, K_B - 1, POOL_PAD)
    out_shape = [jax.ShapeDtypeStruct((n_tiles * T, D_MODEL), F32)] + [st_shape(n) for n in halos]
    out_specs = [pl.BlockSpec((T, D_MODEL), tile_idx)] + [st_spec(n) for n in halos]

    scratch = [
        pltpu.VMEM((T, D_MODEL), BF16),
        pltpu.VMEM((3, T, D_MODEL), F32),
        pltpu.VMEM((T, D_MODEL), F32),
        pltpu.VMEM((T, D_MODEL), F32),
    ] + [pltpu.VMEM((SUBLANES * (n + Q), D_MODEL), F32) for n in halos] + [
        pltpu.VMEM((T, D_MODEL), BF16),
        pltpu.VMEM((T, D_MODEL), BF16),
    ]
    if chained:
        scratch += [pltpu.VMEM((SUBLANES * n, D_MODEL), F32) for n in halos]
    if permute_in:
        scratch.append(pltpu.VMEM((T, D_MODEL), F32))

    return pl.pallas_call(
        functools.partial(_mixer_kernel, q_frames=Q, sps=sps, chained=chained,
                          permute_in=permute_in, pos0=pos0),
        grid=(n_seq, tiles_per_seq),
        in_specs=in_specs,
        out_specs=out_specs,
        out_shape=out_shape,
        scratch_shapes=scratch,
        compiler_params=pltpu.CompilerParams(
            dimension_semantics=("arbitrary", "arbitrary"),
            vmem_limit_bytes=VMEM_LIMIT_BYTES),
        name="mixer_chained" if chained else "mixer_streams",
    )(*args)


def _ffn_kernel(*refs, q_frames, final):
    Q = q_frames
    T = SUBLANES * Q
    n_blocks = T // ROWS
    x_ref, ada_ref, n2g_ref, wfi_ref, wfo_ref, fg_ref, o_ref = refs[:7]
    h_ref, act_ref = refs[7:9]

    n2g = n2g_ref[...]
    for i in range(n_blocks):
        rows = slice(i * ROWS, (i + 1) * ROWS)
        hrow = _rms(x_ref[rows, :]) * n2g * (1.0 + ada_ref[A_S2]) + ada_ref[A_SH2]
        h_ref[rows, :] = hrow.astype(BF16)

    h = h_ref[...]
    for f in range(D_FF // MXU_COLS):
        gate = _dot(h, wfi_ref[:, _colblk(f)])
        up = _dot(h, wfi_ref[:, _colblk(f, D_FF)])
        act_ref[:, _colblk(f)] = (_silu(gate) * up).astype(BF16)

    dst_ref = refs[9] if final else o_ref
    for n in range(N_COLBLK):
        cs = _colblk(n)
        y = _dot(act_ref[...], wfo_ref[:, cs])
        g2 = ada_ref[A_G2, :, cs]
        for i in range(n_blocks):
            rows = slice(i * ROWS, (i + 1) * ROWS)
            dst_ref[rows, cs] = x_ref[rows, cs] + g2 * y[rows, :]

    if final:
        xn_ref = refs[9]
        fg = fg_ref[...]
        for i in range(n_blocks):
            rows = slice(i * ROWS, (i + 1) * ROWS)
            xn_ref[rows, :] = _rms(xn_ref[rows, :]) * fg
        for c in range(N_CHUNKS):
            col = slice(c * LANES, (c + 1) * LANES)
            by_frame = xn_ref[:, col].reshape(Q, SUBLANES, LANES)
            o_ref[:, col] = jnp.swapaxes(by_frame, 0, 1).reshape(T, LANES)


def _ffn_call(x, ada, n2g, wfi, wfo, fg, l, *, n_seq, tiles_per_seq, q_frames, final):
    Q = q_frames
    T = SUBLANES * Q
    n_tiles = n_seq * tiles_per_seq
    tile_idx = lambda b, j: (b * tiles_per_seq + j, 0)
    scratch = [
        pltpu.VMEM((T, D_MODEL), BF16),
        pltpu.VMEM((T, D_FF), BF16),
    ]
    if final:
        scratch.append(pltpu.VMEM((T, D_MODEL), F32))
    return pl.pallas_call(
        functools.partial(_ffn_kernel, q_frames=Q, final=final),
        grid=(n_seq, tiles_per_seq),
        in_specs=[
            pl.BlockSpec((T, D_MODEL), tile_idx),
            pl.BlockSpec((None, 6, ROWS, D_MODEL), lambda b, j: (b, 0, 0, 0)),
            _layer_spec(n2g, l), _layer_spec(wfi, l), _layer_spec(wfo, l),
            pl.BlockSpec(fg.shape, lambda b, j: (0, 0), pipeline_mode=pl.Buffered(1)),
        ],
        out_specs=pl.BlockSpec((T, D_MODEL), tile_idx),
        out_shape=jax.ShapeDtypeStruct((n_tiles * T, D_MODEL), F32),
        scratch_shapes=scratch,
        compiler_params=pltpu.CompilerParams(
            dimension_semantics=("arbitrary", "arbitrary"),
            vmem_limit_bytes=VMEM_LIMIT_BYTES),
        name="ffn_final" if final else "ffn",
    )(x, ada, n2g, wfi, wfo, fg)


def _run_group(x, ada, hists, w_mixer, w_ffn, final_g, *, n_seq, tiles_per_seq,
               q_frames, sps, chained, pos0):
    states = []
    for l in range(DEPTH):
        x, sa, sb, sp = _mixer_call(
            x, ada[l], None if chained else hists[l], w_mixer, l,
            n_seq=n_seq, tiles_per_seq=tiles_per_seq, q_frames=q_frames, sps=sps,
            chained=chained, permute_in=(l == 0), pos0=pos0)
        x = _ffn_call(x, ada[l], *w_ffn, final_g, l, n_seq=n_seq,
                      tiles_per_seq=tiles_per_seq, q_frames=q_frames,
                      final=(l == DEPTH - 1))
        states.append((sa, sb, sp))
    return x, states


def kernel(x_prompt, x_sample, c_prompt, c_sample, cache_conv_a, cache_conv_b, cache_pool, w_ada, b_ada, norm1_g, w_in, w_conv_a, w_out_a, w_conv_b, b_conv_b, ln_b_g, ln_b_b, w_out_b, w_pool, pool_scale, w_o, norm2_g, w_ffn_in, w_ffn_out, final_g):
    bp, sp_len, _ = x_prompt.shape
    bs, ss_len, _ = x_sample.shape
    q_frames = 64
    tile = SUBLANES * q_frames
    assert q_frames >= K_B - 1 and sp_len % tile == 0 and ss_len % q_frames == 0
    sps_s = ss_len // q_frames
    seq_per_tile = SUBLANES // sps_s
    assert SUBLANES % sps_s == 0 and bs % seq_per_tile == 0
    n_sgrp = bs // seq_per_tile

    row = lambda v: v.reshape(DEPTH, 1, -1)
    w_mixer = [row(norm1_g), w_in.astype(BF16), w_conv_a, w_out_a.astype(BF16),
               w_conv_b, row(b_conv_b), row(ln_b_g), row(ln_b_b),
               w_out_b.astype(BF16), w_pool.astype(BF16), row(pool_scale),
               w_o.astype(BF16)]
    w_ffn = [row(norm2_g), w_ffn_in.astype(BF16), w_ffn_out.astype(BF16)]
    fg = final_g.reshape(1, D_MODEL)

    ada = _ada_call(jnp.concatenate([c_prompt, c_sample], axis=0), w_ada, b_ada)
    ada = ada.reshape(DEPTH, bp + bs, 6, D_MODEL)
    ada_p = jnp.broadcast_to(ada[:, :bp, :, None, :], (DEPTH, bp, 6, ROWS, D_MODEL))
    ada_s = ada[:, bp:].reshape(DEPTH, n_sgrp, seq_per_tile, 6, D_MODEL)
    ada_s = jnp.repeat(jnp.transpose(ada_s, (0, 1, 3, 2, 4)), sps_s, axis=3)
    ada_s = jnp.tile(ada_s, (1, 1, 1, ROWS // SUBLANES, 1))

    def frame_major(cache):
        n = cache.shape[2]
        c = cache.reshape(DEPTH, n_sgrp, seq_per_tile, n, D_MODEL)
        return jnp.repeat(jnp.transpose(c, (0, 1, 3, 2, 4)), sps_s, axis=3)
    hists = list(zip(frame_major(cache_conv_a), frame_major(cache_conv_b),
                     frame_major(cache_pool)))

    yp, st_p = _run_group(
        x_prompt.reshape(bp * sp_len, D_MODEL), ada_p, None, w_mixer, w_ffn, fg,
        n_seq=bp, tiles_per_seq=sp_len // tile, q_frames=q_frames,
        sps=SUBLANES, chained=True, pos0=0)
    ys, st_s = _run_group(
        x_sample.reshape(bs * ss_len, D_MODEL), ada_s, hists, w_mixer, w_ffn, fg,
        n_seq=n_sgrp, tiles_per_seq=1, q_frames=q_frames, sps=sps_s, chained=False,
        pos0=PAST_LEN)

    def seq_major(s):
        return s.reshape(bs, s.shape[2], D_MODEL)

    return (yp.reshape(bp, sp_len, D_MODEL), ys.reshape(bs, ss_len, D_MODEL),
            jnp.stack([s[0] for s in st_p]), jnp.stack([s[1] for s in st_p]),
            jnp.stack([s[2] for s in st_p]),
            jnp.stack([seq_major(s[0]) for s in st_s]),
            jnp.stack([seq_major(s[1]) for s in st_s]),
            jnp.stack([seq_major(s[2]) for s in st_s]))
```

```python
import functools

import jax
import jax.numpy as jnp
from jax import lax
from jax.experimental import pallas as pl
from jax.experimental.pallas import tpu as pltpu

F32 = jnp.float32
BF16 = jnp.bfloat16

D_MODEL = 1024
DEPTH = 2
EPS = 1e-6
K_A = 3
K_B = 31
POOL_WINDOWS = (2, 4, 8, 16)
N_POOL = len(POOL_WINDOWS)
POOL_GROUP = D_MODEL // N_POOL
POOL_PAD = max(POOL_WINDOWS) - 1
N_IN = 9 * D_MODEL
D_FF = 2816
PAST_LEN = 1024

SUBLANES = 8
LANES = 128
ROWS = 16
N_CHUNKS = D_MODEL // LANES
MXU_COLS = 256
N_COLBLK = D_MODEL // MXU_COLS
CONV_CHAIN = 16
VMEM_LIMIT_BYTES = 60 * 1024 * 1024

C_BG, C_CG, C_HA, C_GA, C_GB, C_PIN, C_G0, C_G1, C_G2 = (i * D_MODEL for i in range(9))

A_SH1, A_S1, A_G1, A_SH2, A_S2, A_G2 = range(6)


def _sigmoid(x):
    return 0.5 * jnp.tanh(0.5 * x) + 0.5


def _silu(x):
    hx = 0.5 * x
    return hx + hx * jnp.tanh(hx)


def _dot(a, b):
    return jnp.dot(a, b, preferred_element_type=F32)


def _colblk(n, c0=0):
    return slice(c0 + n * MXU_COLS, c0 + (n + 1) * MXU_COLS)


def _rms(x):
    return x * lax.rsqrt(jnp.mean(x * x, axis=-1, keepdims=True) + EPS)


def _ada_kernel(c_ref, w_ref, b_ref, o_ref):
    sc = _silu(c_ref[...]).astype(BF16)
    o_ref[...] = _dot(sc, w_ref[...].astype(BF16)) + b_ref[...]


def _ada_call(c_all, w_ada, b_ada):
    n = c_all.shape[0]
    tn = 1536
    return pl.pallas_call(
        _ada_kernel,
        grid=(DEPTH, 6 * D_MODEL // tn),
        in_specs=[
            pl.BlockSpec((n, D_MODEL), lambda l, j: (0, 0)),
            pl.BlockSpec((None, D_MODEL, tn), lambda l, j: (l, 0, j)),
            pl.BlockSpec((None, 1, tn), lambda l, j: (l, 0, j)),
        ],
        out_specs=pl.BlockSpec((None, n, tn), lambda l, j: (l, 0, j)),
        out_shape=jax.ShapeDtypeStruct((DEPTH, n, 6 * D_MODEL), F32),
        compiler_params=pltpu.CompilerParams(
            dimension_semantics=("arbitrary", "arbitrary"),
            vmem_limit_bytes=VMEM_LIMIT_BYTES),
        name="ada",
    )(c_all, w_ada, b_ada.reshape(DEPTH, 1, 6 * D_MODEL))


def _mixer_kernel(*refs, q_frames, sps, chained, permute_in, pos0):
    Q = q_frames
    T = SUBLANES * Q
    refs = list(refs)
    x_ref, ada_ref = refs[:2]
    refs = refs[2:]
    if not chained:
        hist_a_ref, hist_b_ref, hist_p_ref = refs[:3]
        refs = refs[3:]
    else:
        hist_a_ref = hist_b_ref = hist_p_ref = None
    (n1g_ref, win_ref, wca_ref, woa_ref, wcb_ref, bcb_ref, lng_ref, lnb_ref,
     wob_ref, wpool_ref, pscale_ref, wo_ref) = refs[:12]
    refs = refs[12:]
    xo_ref, sa_ref, sb_ref, sp_ref = refs[:4]
    refs = refs[4:]
    (h_ref, sg_ref, m_ref, cb_ref, eu_ref, ev_ref, ep_ref,
     a_ref, b_ref, c_ref) = refs[:10]
    refs = refs[10:]
    if chained:
        cu_ref, cv_ref, cp_ref = refs[:3]
        refs = refs[3:]
    else:
        cu_ref = cv_ref = cp_ref = None
    xs_ref = refs[0] if permute_in else x_ref

    j = pl.program_id(1)
    n_blocks = T // ROWS

    if permute_in:
        for c in range(N_CHUNKS):
            col = slice(c * LANES, (c + 1) * LANES)
            by_stream = x_ref[:, col].reshape(SUBLANES, Q, LANES)
            xs_ref[:, col] = jnp.swapaxes(by_stream, 0, 1).reshape(T, LANES)

    if chained:
        @pl.when(j == 0)
        def _():
            cu_ref[...] = jnp.zeros_like(cu_ref)
            cv_ref[...] = jnp.zeros_like(cv_ref)
            cp_ref[...] = jnp.zeros_like(cp_ref)

    stream = lax.broadcasted_iota(jnp.int32, (SUBLANES, MXU_COLS), 0)
    first = (stream & (sps - 1)) == 0

    def fill_halo(e_ref, carry_ref, hist_ref, n_halo, cs):
        for g in range(n_halo):
            dst = slice(SUBLANES * g, SUBLANES * (g + 1))
            tail = e_ref[SUBLANES * (Q + g):SUBLANES * (Q + g + 1), cs]
            prev = pltpu.roll(tail, 1, axis=0)
            hist = carry_ref[dst, cs] if chained else hist_ref[g, :, cs]
            e_ref[dst, cs] = jnp.where(first, hist, prev)
            if chained:
                carry_ref[dst, cs] = prev

    n1g = n1g_ref[...]
    for i in range(n_blocks):
        rows = slice(i * ROWS, (i + 1) * ROWS)
        hrow = _rms(xs_ref[rows, :]) * n1g * (1.0 + ada_ref[A_S1]) + ada_ref[A_SH1]
        h_ref[rows, :] = hrow.astype(BF16)

    if pos0 < POOL_PAD:
        rr = lax.broadcasted_iota(jnp.int32, (T, 1), 0)
        frame = j * T + (rr & (SUBLANES - 1)) * Q + (rr >> 3)
        avail = pos0 + frame + 1

    h = h_ref[...]
    pscale = pscale_ref[...]
    for n in range(N_COLBLK):
        cs = _colblk(n)
        proj = lambda c0: _dot(h, win_ref[:, _colblk(n, c0)])

        eu_ref[SUBLANES * (K_A - 1):, cs] = proj(C_CG) * proj(C_HA)
        fill_halo(eu_ref, cu_ref, hist_a_ref, K_A - 1, cs)
        acc = None
        for k in range(K_A):
            t = eu_ref[SUBLANES * k:SUBLANES * k + T, cs] * wca_ref[k:k + 1, cs]
            acc = t if acc is None else acc + t
        a_ref[:, cs] = (proj(C_BG) * acc).astype(BF16)

        ev_ref[SUBLANES * (K_B - 1):, cs] = proj(C_GA) * _sigmoid(proj(C_GB))
        fill_halo(ev_ref, cv_ref, hist_b_ref, K_B - 1, cs)
        for cc in range(MXU_COLS // LANES):
            col = slice(cs.start + cc * LANES, cs.start + (cc + 1) * LANES)
            wk = [jnp.broadcast_to(wcb_ref[k:k + 1, col], (ROWS, LANES)).astype(BF16)
                  for k in range(K_B)]
            n_win = 2 * (Q // 2 - 1) + K_B
            win = [ev_ref[SUBLANES * g:SUBLANES * g + ROWS, col].astype(BF16)
                   for g in range(n_win)]
            bias = jnp.broadcast_to(bcb_ref[:, col], (ROWS, LANES))
            for i in range(Q // 2):
                acc = bias
                for k0 in range(0, K_B, CONV_CHAIN):
                    run = None
                    for k in range(k0, min(k0 + CONV_CHAIN, K_B)):
                        t = win[2 * i + k] * wk[k]
                        run = t if run is None else run + t
                    acc = acc + run.astype(F32)
                cb_ref[i * ROWS:(i + 1) * ROWS, col] = acc

        for gi, c0 in enumerate((C_G0, C_G1, C_G2)):
            sg_ref[gi, :, cs] = _sigmoid(proj(c0))

        ep_ref[SUBLANES * POOL_PAD:, cs] = proj(C_PIN)
        fill_halo(ep_ref, cp_ref, hist_p_ref, POOL_PAD, cs)
        w = POOL_WINDOWS[n]
        cur = ep_ref[SUBLANES * POOL_PAD:, cs]
        acc = cur
        for s in range(1, w):
            acc = acc + ep_ref[SUBLANES * (POOL_PAD - s):SUBLANES * (POOL_PAD - s) + T, cs]
        if pos0 < POOL_PAD:
            inv = 1.0 / jnp.minimum(avail, w).astype(F32)
        else:
            inv = 1.0 / w
        c_ref[:, cs] = (acc * inv - cur).astype(BF16)

    for n in range(N_COLBLK):
        cs = _colblk(n)
        y_c = _dot(c_ref[:, cs], wpool_ref[n])
        m_ref[:, cs] = sg_ref[2, :, cs] * (y_c * pscale[:, cs])
    for n in range(N_COLBLK):
        cs = _colblk(n)
        m_ref[:, cs] += sg_ref[0, :, cs] * _dot(a_ref[...], woa_ref[:, cs])

    lng = lng_ref[...]
    lnb = lnb_ref[...]
    for i in range(n_blocks):
        rows = slice(i * ROWS, (i + 1) * ROWS)
        cb = cb_ref[rows, :]
        mu = jnp.mean(cb, axis=-1, keepdims=True)
        d = cb - mu
        var = jnp.mean(d * d, axis=-1, keepdims=True)
        y = d * lax.rsqrt(var + EPS) * lng + lnb
        b_ref[rows, :] = _silu(y).astype(BF16)

    for n in range(N_COLBLK):
        cs = _colblk(n)
        m = m_ref[:, cs] + sg_ref[1, :, cs] * _dot(b_ref[...], wob_ref[:, cs])
        h_ref[:, cs] = m.astype(BF16)

    for n in range(N_COLBLK):
        cs = _colblk(n)
        out = _dot(h_ref[...], wo_ref[:, cs])
        g1 = ada_ref[A_G1, :, cs]
        for i in range(n_blocks):
            rows = slice(i * ROWS, (i + 1) * ROWS)
            xo_ref[rows, cs] = xs_ref[rows, cs] + g1 * out[rows, :]

    if chained:
        @pl.when(j == pl.num_programs(1) - 1)
        def _():
            for carry_ref, s_ref, n_halo in ((cu_ref, sa_ref, K_A - 1),
                                             (cv_ref, sb_ref, K_B - 1),
                                             (cp_ref, sp_ref, POOL_PAD)):
                for g in range(n_halo):
                    s_ref[g:g + 1, :] = carry_ref[SUBLANES * g:SUBLANES * g + 1, :]
    else:
        for e_ref, s_ref, n_halo in ((eu_ref, sa_ref, K_A - 1),
                                     (ev_ref, sb_ref, K_B - 1),
                                     (ep_ref, sp_ref, POOL_PAD)):
            for s in range(SUBLANES // sps):
                r = sps * s + sps - 1
                for g in range(n_halo):
                    row = SUBLANES * (Q + g) + r
                    s_ref[s, g:g + 1, :] = e_ref[row:row + 1, :]


def _layer_spec(arr, l):
    nd = arr.ndim - 1
    return pl.BlockSpec((None,) + arr.shape[1:], lambda *_: (l,) + (0,) * nd,
                        pipeline_mode=pl.Buffered(1))


def _mixer_call(x, ada, hists, w, l, *, n_seq, tiles_per_seq, q_frames, sps, chained,
                permute_in, pos0):
    Q = q_frames
    T = SUBLANES * Q
    n_tiles = n_seq * tiles_per_seq
    tile_idx = lambda b, j: (b * tiles_per_seq + j, 0)
    in_specs = [pl.BlockSpec((T, D_MODEL), tile_idx)]
    args = [x]
    in_specs.append(pl.BlockSpec((None, 6, ROWS, D_MODEL), lambda b, j: (b, 0, 0, 0)))
    args.append(ada)
    if not chained:
        for hst in hists:
            in_specs.append(pl.BlockSpec((None,) + hst.shape[1:], lambda b, j: (b, 0, 0, 0)))
            args.append(hst)
    for arr in w:
        in_specs.append(_layer_spec(arr, l))
        args.append(arr)

    if chained:
        st_shape = lambda n: jax.ShapeDtypeStruct((n_seq, n, D_MODEL), F32)
        st_spec = lambda n: pl.BlockSpec((None, n, D_MODEL), lambda b, j: (b, 0, 0))
    else:
        spt = SUBLANES // sps
        st_shape = lambda n: jax.ShapeDtypeStruct((n_seq, spt, n, D_MODEL), F32)
        st_spec = lambda n: pl.BlockSpec((None, spt, n, D_MODEL), lambda b, j: (b, 0, 0, 0))
    halos = (K_A - 1, K_B - 1, POOL_PAD)
    out_shape = [jax.ShapeDtypeStruct((n_tiles * T, D_MODEL), F32)] + [st_shape(n) for n in halos]
    out_specs = [pl.BlockSpec((T, D_MODEL), tile_idx)] + [st_spec(n) for n in halos]

    scratch = [
        pltpu.VMEM((T, D_MODEL), BF16),
        pltpu.VMEM((3, T, D_MODEL), F32),
        pltpu.VMEM((T, D_MODEL), F32),
        pltpu.VMEM((T, D_MODEL), F32),
    ] + [pltpu.VMEM((SUBLANES * (n + Q), D_MODEL), F32) for n in halos] + [
        pltpu.VMEM((T, D_MODEL), BF16),
        pltpu.VMEM((T, D_MODEL), BF16),
        pltpu.VMEM((T, D_MODEL), BF16),
    ]
    if chained:
        scratch += [pltpu.VMEM((SUBLANES * n, D_MODEL), F32) for n in halos]
    if permute_in:
        scratch.append(pltpu.VMEM((T, D_MODEL), F32))

    return pl.pallas_call(
        functools.partial(_mixer_kernel, q_frames=Q, sps=sps, chained=chained,
                          permute_in=permute_in, pos0=pos0),
        grid=(n_seq, tiles_per_seq),
        in_specs=in_specs,
        out_specs=out_specs,
        out_shape=out_shape,
        scratch_shapes=scratch,
        compiler_params=pltpu.CompilerParams(
            dimension_semantics=("arbitrary", "arbitrary"),
            vmem_limit_bytes=VMEM_LIMIT_BYTES),
        name="mixer_chained" if chained else "mixer_streams",
    )(*args)


def _ffn_kernel(*refs, q_frames, final):
    Q = q_frames
    T = SUBLANES * Q
    n_blocks = T // ROWS
    x_ref, ada_ref, n2g_ref, wfi_ref, wfo_ref, fg_ref, o_ref = refs[:7]
    h_ref, act_ref = refs[7:9]

    n2g = n2g_ref[...]
    for i in range(n_blocks):
        rows = slice(i * ROWS, (i + 1) * ROWS)
        hrow = _rms(x_ref[rows, :]) * n2g * (1.0 + ada_ref[A_S2]) + ada_ref[A_SH2]
        h_ref[rows, :] = hrow.astype(BF16)

    h = h_ref[...]
    for f in range(D_FF // MXU_COLS):
        gate = _dot(h, wfi_ref[:, _colblk(f)])
        up = _dot(h, wfi_ref[:, _colblk(f, D_FF)])
        act_ref[:, _colblk(f)] = (_silu(gate) * up).astype(BF16)

    dst_ref = refs[9] if final else o_ref
    for n in range(N_COLBLK):
        cs = _colblk(n)
        y = _dot(act_ref[...], wfo_ref[:, cs])
        g2 = ada_ref[A_G2, :, cs]
        for i in range(n_blocks):
            rows = slice(i * ROWS, (i + 1) * ROWS)
            dst_ref[rows, cs] = x_ref[rows, cs] + g2 * y[rows, :]

    if final:
        xn_ref = refs[9]
        fg = fg_ref[...]
        for i in range(n_blocks):
            rows = slice(i * ROWS, (i + 1) * ROWS)
            xn_ref[rows, :] = _rms(xn_ref[rows, :]) * fg
        for c in range(N_CHUNKS):
            col = slice(c * LANES, (c + 1) * LANES)
            by_frame = xn_ref[:, col].reshape(Q, SUBLANES, LANES)
            o_ref[:, col] = jnp.swapaxes(by_frame, 0, 1).reshape(T, LANES)


def _ffn_call(x, ada, n2g, wfi, wfo, fg, l, *, n_seq, tiles_per_seq, q_frames, final):
    Q = q_frames
    T = SUBLANES * Q
    n_tiles = n_seq * tiles_per_seq
    tile_idx = lambda b, j: (b * tiles_per_seq + j, 0)
    scratch = [
        pltpu.VMEM((T, D_MODEL), BF16),
        pltpu.VMEM((T, D_FF), BF16),
    ]
    if final:
        scratch.append(pltpu.VMEM((T, D_MODEL), F32))
    return pl.pallas_call(
        functools.partial(_ffn_kernel, q_frames=Q, final=final),
        grid=(n_seq, tiles_per_seq),
        in_specs=[
            pl.BlockSpec((T, D_MODEL), tile_idx),
            pl.BlockSpec((None, 6, ROWS, D_MODEL), lambda b, j: (b, 0, 0, 0)),
            _layer_spec(n2g, l), _layer_spec(wfi, l), _layer_spec(wfo, l),
            pl.BlockSpec(fg.shape, lambda b, j: (0, 0), pipeline_mode=pl.Buffered(1)),
        ],
        out_specs=pl.BlockSpec((T, D_MODEL), tile_idx),
        out_shape=jax.ShapeDtypeStruct((n_tiles * T, D_MODEL), F32),
        scratch_shapes=scratch,
        compiler_params=pltpu.CompilerParams(
            dimension_semantics=("arbitrary", "arbitrary"),
            vmem_limit_bytes=VMEM_LIMIT_BYTES),
        name="ffn_final" if final else "ffn",
    )(x, ada, n2g, wfi, wfo, fg)


def _run_group(x, ada, hists, w_mixer, w_ffn, final_g, *, n_seq, tiles_per_seq,
               q_frames, sps, chained, pos0):
    states = []
    for l in range(DEPTH):
        x, sa, sb, sp = _mixer_call(
            x, ada[l], None if chained else hists[l], w_mixer, l,
            n_seq=n_seq, tiles_per_seq=tiles_per_seq, q_frames=q_frames, sps=sps,
            chained=chained, permute_in=(l == 0), pos0=pos0)
        x = _ffn_call(x, ada[l], *w_ffn, final_g, l, n_seq=n_seq,
                      tiles_per_seq=tiles_per_seq, q_frames=q_frames,
                      final=(l == DEPTH - 1))
        states.append((sa, sb, sp))
    return x, states


def kernel(x_prompt, x_sample, c_prompt, c_sample, cache_conv_a, cache_conv_b, cache_pool, w_ada, b_ada, norm1_g, w_in, w_conv_a, w_out_a, w_conv_b, b_conv_b, ln_b_g, ln_b_b, w_out_b, w_pool, pool_scale, w_o, norm2_g, w_ffn_in, w_ffn_out, final_g):
    bp, sp_len, _ = x_prompt.shape
    bs, ss_len, _ = x_sample.shape
    q_frames = 64
    tile = SUBLANES * q_frames
    assert q_frames >= K_B - 1 and sp_len % tile == 0 and ss_len % q_frames == 0
    sps_s = ss_len // q_frames
    seq_per_tile = SUBLANES // sps_s
    assert SUBLANES % sps_s == 0 and bs % seq_per_tile == 0
    n_sgrp = bs // seq_per_tile

    row = lambda v: v.reshape(DEPTH, 1, -1)
    w_mixer = [row(norm1_g), w_in.astype(BF16), w_conv_a, w_out_a.astype(BF16),
               w_conv_b, row(b_conv_b), row(ln_b_g), row(ln_b_b),
               w_out_b.astype(BF16), w_pool.astype(BF16), row(pool_scale),
               w_o.astype(BF16)]
    w_ffn = [row(norm2_g), w_ffn_in.astype(BF16), w_ffn_out.astype(BF16)]
    fg = final_g.reshape(1, D_MODEL)

    ada = _ada_call(jnp.concatenate([c_prompt, c_sample], axis=0), w_ada, b_ada)
    ada = ada.reshape(DEPTH, bp + bs, 6, D_MODEL)
    ada_p = jnp.broadcast_to(ada[:, :bp, :, None, :], (DEPTH, bp, 6, ROWS, D_MODEL))
    ada_s = ada[:, bp:].reshape(DEPTH, n_sgrp, seq_per_tile, 6, D_MODEL)
    ada_s = jnp.repeat(jnp.transpose(ada_s, (0, 1, 3, 2, 4)), sps_s, axis=3)
    ada_s = jnp.tile(ada_s, (1, 1, 1, ROWS // SUBLANES, 1))

    def frame_major(cache):
        n = cache.shape[2]
        c = cache.reshape(DEPTH, n_sgrp, seq_per_tile, n, D_MODEL)
        return jnp.repeat(jnp.transpose(c, (0, 1, 3, 2, 4)), sps_s, axis=3)
    hists = list(zip(frame_major(cache_conv_a), frame_major(cache_conv_b),
                     frame_major(cache_pool)))

    yp, st_p = _run_group(
        x_prompt.reshape(bp * sp_len, D_MODEL), ada_p, None, w_mixer, w_ffn, fg,
        n_seq=bp, tiles_per_seq=sp_len // tile, q_frames=q_frames,
        sps=SUBLANES, chained=True, pos0=0)
    ys, st_s = _run_group(
        x_sample.reshape(bs * ss_len, D_MODEL), ada_s, hists, w_mixer, w_ffn, fg,
        n_seq=n_sgrp, tiles_per_seq=1, q_frames=q_frames, sps=sps_s, chained=False,
        pos0=PAST_LEN)

    def seq_major(s):
        return s.reshape(bs, s.shape[2], D_MODEL)

    return (yp.reshape(bp, sp_len, D_MODEL), ys.reshape(bs, ss_len, D_MODEL),
            jnp.stack([s[0] for s in st_p]), jnp.stack([s[1] for s in st_p]),
            jnp.stack([s[2] for s in st_p]),
            jnp.stack([seq_major(s[0]) for s in st_s]),
            jnp.stack([seq_major(s[1]) for s in st_s]),
            jnp.stack([seq_major(s[2]) for s in st_s]))
```

```python
import functools

import jax
import jax.numpy as jnp
from jax import lax
from jax.experimental import pallas as pl
from jax.experimental.pallas import tpu as pltpu

F32 = jnp.float32
BF16 = jnp.bfloat16

D_MODEL = 1024
DEPTH = 2
EPS = 1e-6
K_A = 3
K_B = 31
POOL_WINDOWS = (2, 4, 8, 16)
N_POOL = len(POOL_WINDOWS)
POOL_GROUP = D_MODEL // N_POOL
POOL_PAD = max(POOL_WINDOWS) - 1
N_IN = 9 * D_MODEL
D_FF = 2816
PAST_LEN = 1024

SUBLANES = 8
LANES = 128
ROWS = 16
N_CHUNKS = D_MODEL // LANES
MXU_COLS = 256
N_COLBLK = D_MODEL // MXU_COLS
CONV_CHAIN = 16
VMEM_LIMIT_BYTES = 60 * 1024 * 1024

C_BG, C_CG, C_HA, C_GA, C_GB, C_PIN, C_G0, C_G1, C_G2 = (i * D_MODEL for i in range(9))

A_SH1, A_S1, A_G1, A_SH2, A_S2, A_G2 = range(6)


def _sigmoid(x):
    return 0.5 * jnp.tanh(0.5 * x) + 0.5


def _silu(x):
    hx = 0.5 * x
    return hx + hx * jnp.tanh(hx)


def _dot(a, b):
    return jnp.dot(a, b, preferred_element_type=F32)


def _colblk(n, c0=0):
    return slice(c0 + n * MXU_COLS, c0 + (n + 1) * MXU_COLS)


def _rms(x):
    return x * lax.rsqrt(jnp.mean(x * x, axis=-1, keepdims=True) + EPS)


def _ada_kernel(c_ref, w_ref, b_ref, o_ref):
    sc = _silu(c_ref[...]).astype(BF16)
    o_ref[...] = _dot(sc, w_ref[...].astype(BF16)) + b_ref[...]


def _ada_call(c_all, w_ada, b_ada):
    n = c_all.shape[0]
    tn = 1536
    return pl.pallas_call(
        _ada_kernel,
        grid=(DEPTH, 6 * D_MODEL // tn),
        in_specs=[
            pl.BlockSpec((n, D_MODEL), lambda l, j: (0, 0)),
            pl.BlockSpec((None, D_MODEL, tn), lambda l, j: (l, 0, j)),
            pl.BlockSpec((None, 1, tn), lambda l, j: (l, 0, j)),
        ],
        out_specs=pl.BlockSpec((None, n, tn), lambda l, j: (l, 0, j)),
        out_shape=jax.ShapeDtypeStruct((DEPTH, n, 6 * D_MODEL), F32),
        compiler_params=pltpu.CompilerParams(
            dimension_semantics=("arbitrary", "arbitrary"),
            vmem_limit_bytes=VMEM_LIMIT_BYTES),
        name="ada",
    )(c_all, w_ada, b_ada.reshape(DEPTH, 1, 6 * D_MODEL))


def _mixer_kernel(*refs, q_frames, sps, chained, permute_in, pos0):
    Q = q_frames
    T = SUBLANES * Q
    refs = list(refs)
    x_ref, ada_ref = refs[:2]
    refs = refs[2:]
    if not chained:
        hist_a_ref, hist_b_ref, hist_p_ref = refs[:3]
        refs = refs[3:]
    else:
        hist_a_ref = hist_b_ref = hist_p_ref = None
    (n1g_ref, win_ref, wca_ref, woa_ref, wcb_ref, bcb_ref, lng_ref, lnb_ref,
     wob_ref, wpool_ref, pscale_ref, wo_ref) = refs[:12]
    refs = refs[12:]
    xo_ref, sa_ref, sb_ref, sp_ref = refs[:4]
    refs = refs[4:]
    (h_ref, sg_ref, m_ref, cb_ref, eu_ref, ev_ref, ep_ref,
     a_ref, b_ref, c_ref) = refs[:10]
    refs = refs[10:]
    if chained:
        cu_ref, cv_ref, cp_ref = refs[:3]
        refs = refs[3:]
    else:
        cu_ref = cv_ref = cp_ref = None
    xs_ref = refs[0] if permute_in else x_ref

    j = pl.program_id(1)
    n_blocks = T // ROWS

    if permute_in:
        for c in range(N_CHUNKS):
            col = slice(c * LANES, (c + 1) * LANES)
            by_stream = x_ref[:, col].reshape(SUBLANES, Q, LANES)
            xs_ref[:, col] = jnp.swapaxes(by_stream, 0, 1).reshape(T, LANES)

    if chained:
        @pl.when(j == 0)
        def _():
            cu_ref[...] = jnp.zeros_like(cu_ref)
            cv_ref[...] = jnp.zeros_like(cv_ref)
            cp_ref[...] = jnp.zeros_like(cp_ref)

    stream = lax.broadcasted_iota(jnp.int32, (SUBLANES, MXU_COLS), 0)
    first = (stream & (sps - 1)) == 0

    def fill_halo(e_ref, carry_ref, hist_ref, n_halo, cs):
        for g in range(n_halo):
            dst = slice(SUBLANES * g, SUBLANES * (g + 1))
            tail = e_ref[SUBLANES * (Q + g):SUBLANES * (Q + g + 1), cs]
            prev = pltpu.roll(tail, 1, axis=0)
            hist = carry_ref[dst, cs] if chained else hist_ref[g, :, cs]
            e_ref[dst, cs] = jnp.where(first, hist, prev)
            if chained:
                carry_ref[dst, cs] = prev

    scale1 = n1g_ref[...] * (1.0 + ada_ref[A_S1])
    for i in range(n_blocks):
        rows = slice(i * ROWS, (i + 1) * ROWS)
        hrow = _rms(xs_ref[rows, :]) * scale1 + ada_ref[A_SH1]
        h_ref[rows, :] = hrow.astype(BF16)

    if pos0 < POOL_PAD:
        rr = lax.broadcasted_iota(jnp.int32, (T, 1), 0)
        frame = j * T + (rr & (SUBLANES - 1)) * Q + (rr >> 3)
        avail = pos0 + frame + 1

    h = h_ref[...]
    pscale = pscale_ref[...]
    for n in range(N_COLBLK):
        cs = _colblk(n)
        proj = lambda c0: _dot(h, win_ref[:, _colblk(n, c0)])

        eu_ref[SUBLANES * (K_A - 1):, cs] = proj(C_CG) * proj(C_HA)
        fill_halo(eu_ref, cu_ref, hist_a_ref, K_A - 1, cs)
        acc = None
        for k in range(K_A):
            t = eu_ref[SUBLANES * k:SUBLANES * k + T, cs] * wca_ref[k:k + 1, cs]
            acc = t if acc is None else acc + t
        a_ref[:, cs] = (proj(C_BG) * acc).astype(BF16)

        ev_ref[SUBLANES * (K_B - 1):, cs] = proj(C_GA) * _sigmoid(proj(C_GB))
        fill_halo(ev_ref, cv_ref, hist_b_ref, K_B - 1, cs)
        for cc in range(MXU_COLS // LANES):
            col = slice(cs.start + cc * LANES, cs.start + (cc + 1) * LANES)
            wk = [jnp.broadcast_to(wcb_ref[k:k + 1, col], (ROWS, LANES)).astype(BF16)
                  for k in range(K_B)]
            n_win = 2 * (Q // 2 - 1) + K_B
            win = [ev_ref[SUBLANES * g:SUBLANES * g + ROWS, col].astype(BF16)
                   for g in range(n_win)]
            bias = jnp.broadcast_to(bcb_ref[:, col], (ROWS, LANES))
            for i in range(Q // 2):
                acc = bias
                for k0 in range(0, K_B, CONV_CHAIN):
                    run = None
                    for k in range(k0, min(k0 + CONV_CHAIN, K_B)):
                        t = win[2 * i + k] * wk[k]
                        run = t if run is None else run + t
                    acc = acc + run.astype(F32)
                cb_ref[i * ROWS:(i + 1) * ROWS, col] = acc

        for gi, c0 in enumerate((C_G0, C_G1, C_G2)):
            sg_ref[gi, :, cs] = _sigmoid(proj(c0))

        ep_ref[SUBLANES * POOL_PAD:, cs] = proj(C_PIN)
        fill_halo(ep_ref, cp_ref, hist_p_ref, POOL_PAD, cs)
        w = POOL_WINDOWS[n]
        cur = ep_ref[SUBLANES * POOL_PAD:, cs]
        acc = cur
        for s in range(1, w):
            acc = acc + ep_ref[SUBLANES * (POOL_PAD - s):SUBLANES * (POOL_PAD - s) + T, cs]
        if pos0 < POOL_PAD:
            inv = 1.0 / jnp.minimum(avail, w).astype(F32)
        else:
            inv = 1.0 / w
        c_ref[:, cs] = (acc * inv - cur).astype(BF16)

    for n in range(N_COLBLK):
        cs = _colblk(n)
        y_c = _dot(c_ref[:, cs], wpool_ref[n])
        m_ref[:, cs] = sg_ref[2, :, cs] * (y_c * pscale[:, cs])
    for n in range(N_COLBLK):
        cs = _colblk(n)
        m_ref[:, cs] += sg_ref[0, :, cs] * _dot(a_ref[...], woa_ref[:, cs])

    lng = lng_ref[...]
    lnb = lnb_ref[...]
    for i in range(n_blocks):
        rows = slice(i * ROWS, (i + 1) * ROWS)
        cb = cb_ref[rows, :]
        mu = jnp.mean(cb, axis=-1, keepdims=True)
        d = cb - mu
        var = jnp.mean(d * d, axis=-1, keepdims=True)
        y = d * lax.rsqrt(var + EPS) * lng + lnb
        b_ref[rows, :] = _silu(y).astype(BF16)

    for n in range(N_COLBLK):
        cs = _colblk(n)
        m = m_ref[:, cs] + sg_ref[1, :, cs] * _dot(b_ref[...], wob_ref[:, cs])
        h_ref[:, cs] = m.astype(BF16)

    for n in range(N_COLBLK):
        cs = _colblk(n)
        out = _dot(h_ref[...], wo_ref[:, cs])
        g1 = ada_ref[A_G1, :, cs]
        for i in range(n_blocks):
            rows = slice(i * ROWS, (i + 1) * ROWS)
            xo_ref[rows, cs] = xs_ref[rows, cs] + g1 * out[rows, :]

    if chained:
        @pl.when(j == pl.num_programs(1) - 1)
        def _():
            for carry_ref, s_ref, n_halo in ((cu_ref, sa_ref, K_A - 1),
                                             (cv_ref, sb_ref, K_B - 1),
                                             (cp_ref, sp_ref, POOL_PAD)):
                for g in range(n_halo):
                    s_ref[g:g + 1, :] = carry_ref[SUBLANES * g:SUBLANES * g + 1, :]
    else:
        for e_ref, s_ref, n_halo in ((eu_ref, sa_ref, K_A - 1),
                                     (ev_ref, sb_ref, K_B - 1),
                                     (ep_ref, sp_ref, POOL_PAD)):
            for s in range(SUBLANES // sps):
                r = sps * s + sps - 1
                for g in range(n_halo):
                    row = SUBLANES * (Q + g) + r
                    s_ref[s, g:g + 1, :] = e_ref[row:row + 1, :]


def _layer_spec(arr, l):
    nd = arr.ndim - 1
    return pl.BlockSpec((None,) + arr.shape[1:], lambda *_: (l,) + (0,) * nd,
                        pipeline_mode=pl.Buffered(1))


def _mixer_call(x, ada, hists, w, l, *, n_seq, tiles_per_seq, q_frames, sps, chained,
                permute_in, pos0):
    Q = q_frames
    T = SUBLANES * Q
    n_tiles = n_seq * tiles_per_seq
    tile_idx = lambda b, j: (b * tiles_per_seq + j, 0)
    in_specs = [pl.BlockSpec((T, D_MODEL), tile_idx)]
    args = [x]
    in_specs.append(pl.BlockSpec((None, 6, ROWS, D_MODEL), lambda b, j: (b, 0, 0, 0)))
    args.append(ada)
    if not chained:
        for hst in hists:
            in_specs.append(pl.BlockSpec((None,) + hst.shape[1:], lambda b, j: (b, 0, 0, 0)))
            args.append(hst)
    for arr in w:
        in_specs.append(_layer_spec(arr, l))
        args.append(arr)

    if chained:
        st_shape = lambda n: jax.ShapeDtypeStruct((n_seq, n, D_MODEL), F32)
        st_spec = lambda n: pl.BlockSpec((None, n, D_MODEL), lambda b, j: (b, 0, 0))
    else:
        spt = SUBLANES // sps
        st_shape = lambda n: jax.ShapeDtypeStruct((n_seq, spt, n, D_MODEL), F32)
        st_spec = lambda n: pl.BlockSpec((None, spt, n, D_MODEL), lambda b, j: (b, 0, 0, 0))
    halos = (K_A - 1, K_B - 1, POOL_PAD)
    out_shape = [jax.ShapeDtypeStruct((n_tiles * T, D_MODEL), F32)] + [st_shape(n) for n in halos]
    out_specs = [pl.BlockSpec((T, D_MODEL), tile_idx)] + [st_spec(n) for n in halos]

    scratch = [
        pltpu.VMEM((T, D_MODEL), BF16),
        pltpu.VMEM((3, T, D_MODEL), F32),
        pltpu.VMEM((T, D_MODEL), F32),
        pltpu.VMEM((T, D_MODEL), F32),
    ] + [pltpu.VMEM((SUBLANES * (n + Q), D_MODEL), F32) for n in halos] + [
        pltpu.VMEM((T, D_MODEL), BF16),
        pltpu.VMEM((T, D_MODEL), BF16),
        pltpu.VMEM((T, D_MODEL), BF16),
    ]
    if chained:
        scratch += [pltpu.VMEM((SUBLANES * n, D_MODEL), F32) for n in halos]
    if permute_in:
        scratch.append(pltpu.VMEM((T, D_MODEL), F32))

    return pl.pallas_call(
        functools.partial(_mixer_kernel, q_frames=Q, sps=sps, chained=chained,
                          permute_in=permute_in, pos0=pos0),
        grid=(n_seq, tiles_per_seq),
        in_specs=in_specs,
        out_specs=out_specs,
        out_shape=out_shape,
        scratch_shapes=scratch,
        compiler_params=pltpu.CompilerParams(
            dimension_semantics=("arbitrary", "arbitrary"),
            vmem_limit_bytes=VMEM_LIMIT_BYTES),
        name="mixer_chained" if chained else "mixer_streams",
    )(*args)


def _ffn_kernel(*refs, q_frames, final):
    Q = q_frames
    T = SUBLANES * Q
    n_blocks = T // ROWS
    x_ref, ada_ref, n2g_ref, wfi_ref, wfo_ref, fg_ref, o_ref = refs[:7]
    h_ref, act_ref = refs[7:9]

    scale2 = n2g_ref[...] * (1.0 + ada_ref[A_S2])
    for i in range(n_blocks):
        rows = slice(i * ROWS, (i + 1) * ROWS)
        hrow = _rms(x_ref[rows, :]) * scale2 + ada_ref[A_SH2]
        h_ref[rows, :] = hrow.astype(BF16)

    h = h_ref[...]
    for f in range(D_FF // MXU_COLS):
        gate = _dot(h, wfi_ref[:, _colblk(f)])
        up = _dot(h, wfi_ref[:, _colblk(f, D_FF)])
        act_ref[:, _colblk(f)] = (_silu(gate) * up).astype(BF16)

    dst_ref = refs[9] if final else o_ref
    for n in range(N_COLBLK):
        cs = _colblk(n)
        y = _dot(act_ref[...], wfo_ref[:, cs])
        g2 = ada_ref[A_G2, :, cs]
        for i in range(n_blocks):
            rows = slice(i * ROWS, (i + 1) * ROWS)
            dst_ref[rows, cs] = x_ref[rows, cs] + g2 * y[rows, :]

    if final:
        xn_ref = refs[9]
        fg = fg_ref[...]
        for i in range(n_blocks):
            rows = slice(i * ROWS, (i + 1) * ROWS)
            xn_ref[rows, :] = _rms(xn_ref[rows, :]) * fg
        for c in range(N_CHUNKS):
            col = slice(c * LANES, (c + 1) * LANES)
            by_frame = xn_ref[:, col].reshape(Q, SUBLANES, LANES)
            o_ref[:, col] = jnp.swapaxes(by_frame, 0, 1).reshape(T, LANES)


def _ffn_call(x, ada, n2g, wfi, wfo, fg, l, *, n_seq, tiles_per_seq, q_frames, final):
    Q = q_frames
    T = SUBLANES * Q
    n_tiles = n_seq * tiles_per_seq
    tile_idx = lambda b, j: (b * tiles_per_seq + j, 0)
    scratch = [
        pltpu.VMEM((T, D_MODEL), BF16),
        pltpu.VMEM((T, D_FF), BF16),
    ]
    if final:
        scratch.append(pltpu.VMEM((T, D_MODEL), F32))
    return pl.pallas_call(
        functools.partial(_ffn_kernel, q_frames=Q, final=final),
        grid=(n_seq, tiles_per_seq),
        in_specs=[
            pl.BlockSpec((T, D_MODEL), tile_idx),
            pl.BlockSpec((None, 6, ROWS, D_MODEL), lambda b, j: (b, 0, 0, 0)),
            _layer_spec(n2g, l), _layer_spec(wfi, l), _layer_spec(wfo, l),
            pl.BlockSpec(fg.shape, lambda b, j: (0, 0), pipeline_mode=pl.Buffered(1)),
        ],
        out_specs=pl.BlockSpec((T, D_MODEL), tile_idx),
        out_shape=jax.ShapeDtypeStruct((n_tiles * T, D_MODEL), F32),
        scratch_shapes=scratch,
        compiler_params=pltpu.CompilerParams(
            dimension_semantics=("arbitrary", "arbitrary"),
            vmem_limit_bytes=VMEM_LIMIT_BYTES),
        name="ffn_final" if final else "ffn",
    )(x, ada, n2g, wfi, wfo, fg)


def _run_group(x, ada, hists, w_mixer, w_ffn, final_g, *, n_seq, tiles_per_seq,
               q_frames, sps, chained, pos0):
    states = []
    for l in range(DEPTH):
        x, sa, sb, sp = _mixer_call(
            x, ada[l], None if chained else hists[l], w_mixer, l,
            n_seq=n_seq, tiles_per_seq=tiles_per_seq, q_frames=q_frames, sps=sps,
            chained=chained, permute_in=(l == 0), pos0=pos0)
        x = _ffn_call(x, ada[l], *w_ffn, final_g, l, n_seq=n_seq,
                      tiles_per_seq=tiles_per_seq, q_frames=q_frames,
                      final=(l == DEPTH - 1))
        states.append((sa, sb, sp))
    return x, states


def kernel(x_prompt, x_sample, c_prompt, c_sample, cache_conv_a, cache_conv_b, cache_pool, w_ada, b_ada, norm1_g, w_in, w_conv_a, w_out_a, w_conv_b, b_conv_b, ln_b_g, ln_b_b, w_out_b, w_pool, pool_scale, w_o, norm2_g, w_ffn_in, w_ffn_out, final_g):
    bp, sp_len, _ = x_prompt.shape
    bs, ss_len, _ = x_sample.shape
    q_frames = 64
    tile = SUBLANES * q_frames
    assert q_frames >= K_B - 1 and sp_len % tile == 0 and ss_len % q_frames == 0
    sps_s = ss_len // q_frames
    seq_per_tile = SUBLANES // sps_s
    assert SUBLANES % sps_s == 0 and bs % seq_per_tile == 0
    n_sgrp = bs // seq_per_tile

    row = lambda v: v.reshape(DEPTH, 1, -1)
    w_mixer = [row(norm1_g), w_in.astype(BF16), w_conv_a, w_out_a.astype(BF16),
               w_conv_b, row(b_conv_b), row(ln_b_g), row(ln_b_b),
               w_out_b.astype(BF16), w_pool.astype(BF16), row(pool_scale),
               w_o.astype(BF16)]
    w_ffn = [row(norm2_g), w_ffn_in.astype(BF16), w_ffn_out.astype(BF16)]
    fg = final_g.reshape(1, D_MODEL)

    ada = _ada_call(jnp.concatenate([c_prompt, c_sample], axis=0), w_ada, b_ada)
    ada = ada.reshape(DEPTH, bp + bs, 6, D_MODEL)
    ada_p = jnp.broadcast_to(ada[:, :bp, :, None, :], (DEPTH, bp, 6, ROWS, D_MODEL))
    ada_s = ada[:, bp:].reshape(DEPTH, n_sgrp, seq_per_tile, 6, D_MODEL)
    ada_s = jnp.repeat(jnp.transpose(ada_s, (0, 1, 3, 2, 4)), sps_s, axis=3)
    ada_s = jnp.tile(ada_s, (1, 1, 1, ROWS // SUBLANES, 1))

    def frame_major(cache):
        n = cache.shape[2]
        c = cache.reshape(DEPTH, n_sgrp, seq_per_tile, n, D_MODEL)
        return jnp.repeat(jnp.transpose(c, (0, 1, 3, 2, 4)), sps_s, axis=3)
    hists = list(zip(frame_major(cache_conv_a), frame_major(cache_conv_b),
                     frame_major(cache_pool)))

    yp, st_p = _run_group(
        x_prompt.reshape(bp * sp_len, D_MODEL), ada_p, None, w_mixer, w_ffn, fg,
        n_seq=bp, tiles_per_seq=sp_len // tile, q_frames=q_frames,
        sps=SUBLANES, chained=True, pos0=0)
    ys, st_s = _run_group(
        x_sample.reshape(bs * ss_len, D_MODEL), ada_s, hists, w_mixer, w_ffn, fg,
        n_seq=n_sgrp, tiles_per_seq=1, q_frames=q_frames, sps=sps_s, chained=False,
        pos0=PAST_LEN)

    def seq_major(s):
        return s.reshape(bs, s.shape[2], D_MODEL)

    return (yp.reshape(bp, sp_len, D_MODEL), ys.reshape(bs, ss_len, D_MODEL),
            jnp.stack([s[0] for s in st_p]), jnp.stack([s[1] for s in st_p]),
            jnp.stack([s[2] for s in st_p]),
            jnp.stack([seq_major(s[0]) for s in st_s]),
            jnp.stack([seq_major(s[1]) for s in st_s]),
            jnp.stack([seq_major(s[2]) for s in st_s]))
```

```python
import functools

import jax
import jax.numpy as jnp
from jax import lax
from jax.experimental import pallas as pl
from jax.experimental.pallas import tpu as pltpu

F32 = jnp.float32
BF16 = jnp.bfloat16

D_MODEL = 1024
DEPTH = 2
EPS = 1e-6
K_A = 3
K_B = 31
POOL_WINDOWS = (2, 4, 8, 16)
N_POOL = len(POOL_WINDOWS)
POOL_GROUP = D_MODEL // N_POOL
POOL_PAD = max(POOL_WINDOWS) - 1
N_IN = 9 * D_MODEL
D_FF = 2816
PAST_LEN = 1024

SUBLANES = 8
LANES = 128
ROWS = 16
N_CHUNKS = D_MODEL // LANES
MXU_COLS = 256
N_COLBLK = D_MODEL // MXU_COLS
CONV_CHAIN = 16
VMEM_LIMIT_BYTES = 60 * 1024 * 1024

C_BG, C_CG, C_HA, C_GA, C_GB, C_PIN, C_G0, C_G1, C_G2 = (i * D_MODEL for i in range(9))

A_SH1, A_S1, A_G1, A_SH2, A_S2, A_G2 = range(6)


def _sigmoid(x):
    return 0.5 * jnp.tanh(0.5 * x) + 0.5


def _silu(x):
    hx = 0.5 * x
    return hx + hx * jnp.tanh(hx)


def _dot(a, b):
    return jnp.dot(a, b, preferred_element_type=F32)


def _colblk(n, c0=0):
    return slice(c0 + n * MXU_COLS, c0 + (n + 1) * MXU_COLS)


def _rms(x):
    return x * lax.rsqrt(jnp.mean(x * x, axis=-1, keepdims=True) + EPS)


def _ada_kernel(c_ref, w_ref, b_ref, o_ref):
    sc = _silu(c_ref[...]).astype(BF16)
    o_ref[...] = _dot(sc, w_ref[...].astype(BF16)) + b_ref[...]


def _ada_call(c_all, w_ada, b_ada):
    n = c_all.shape[0]
    tn = 1536
    return pl.pallas_call(
        _ada_kernel,
        grid=(DEPTH, 6 * D_MODEL // tn),
        in_specs=[
            pl.BlockSpec((n, D_MODEL), lambda l, j: (0, 0)),
            pl.BlockSpec((None, D_MODEL, tn), lambda l, j: (l, 0, j)),
            pl.BlockSpec((None, 1, tn), lambda l, j: (l, 0, j)),
        ],
        out_specs=pl.BlockSpec((None, n, tn), lambda l, j: (l, 0, j)),
        out_shape=jax.ShapeDtypeStruct((DEPTH, n, 6 * D_MODEL), F32),
        compiler_params=pltpu.CompilerParams(
            dimension_semantics=("arbitrary", "arbitrary"),
            vmem_limit_bytes=VMEM_LIMIT_BYTES),
        name="ada",
    )(c_all, w_ada, b_ada.reshape(DEPTH, 1, 6 * D_MODEL))


def _mixer_kernel(*refs, q_frames, sps, chained, permute_in, pos0):
    Q = q_frames
    T = SUBLANES * Q
    refs = list(refs)
    x_ref, ada_ref = refs[:2]
    refs = refs[2:]
    if not chained:
        hist_a_ref, hist_b_ref, hist_p_ref = refs[:3]
        refs = refs[3:]
    else:
        hist_a_ref = hist_b_ref = hist_p_ref = None
    (n1g_ref, win_ref, wca_ref, woa_ref, wcb_ref, bcb_ref, lng_ref, lnb_ref,
     wob_ref, wpool_ref, pscale_ref, wo_ref) = refs[:12]
    refs = refs[12:]
    xo_ref, sa_ref, sb_ref, sp_ref = refs[:4]
    refs = refs[4:]
    (h_ref, sg_ref, m_ref, cb_ref, eu_ref, ev_ref, ep_ref,
     a_ref, b_ref, c_ref) = refs[:10]
    refs = refs[10:]
    if chained:
        cu_ref, cv_ref, cp_ref = refs[:3]
        refs = refs[3:]
    else:
        cu_ref = cv_ref = cp_ref = None
    xs_ref = refs[0] if permute_in else x_ref

    j = pl.program_id(1)
    n_blocks = T // ROWS

    if permute_in:
        for c in range(N_CHUNKS):
            col = slice(c * LANES, (c + 1) * LANES)
            by_stream = x_ref[:, col].reshape(SUBLANES, Q, LANES)
            xs_ref[:, col] = jnp.swapaxes(by_stream, 0, 1).reshape(T, LANES)

    if chained:
        @pl.when(j == 0)
        def _():
            cu_ref[...] = jnp.zeros_like(cu_ref)
            cv_ref[...] = jnp.zeros_like(cv_ref)
            cp_ref[...] = jnp.zeros_like(cp_ref)

    stream = lax.broadcasted_iota(jnp.int32, (SUBLANES, MXU_COLS), 0)
    first = (stream & (sps - 1)) == 0

    def fill_halo(e_ref, carry_ref, hist_ref, n_halo, cs):
        for g in range(n_halo):
            dst = slice(SUBLANES * g, SUBLANES * (g + 1))
            tail = e_ref[SUBLANES * (Q + g):SUBLANES * (Q + g + 1), cs]
            prev = pltpu.roll(tail, 1, axis=0)
            hist = carry_ref[dst, cs] if chained else hist_ref[g, :, cs]
            e_ref[dst, cs] = jnp.where(first, hist, prev)
            if chained:
                carry_ref[dst, cs] = prev

    scale1 = n1g_ref[...] * (1.0 + ada_ref[A_S1])
    for i in range(n_blocks):
        rows = slice(i * ROWS, (i + 1) * ROWS)
        hrow = _rms(xs_ref[rows, :]) * scale1 + ada_ref[A_SH1]
        h_ref[rows, :] = hrow.astype(BF16)

    if pos0 < POOL_PAD:
        rr = lax.broadcasted_iota(jnp.int32, (T, 1), 0)
        frame = j * T + (rr & (SUBLANES - 1)) * Q + (rr >> 3)
        avail = pos0 + frame + 1

    h = h_ref[...]
    pscale = pscale_ref[...]
    for n in range(N_COLBLK):
        cs = _colblk(n)
        proj = lambda c0: _dot(h, win_ref[:, _colblk(n, c0)])

        eu_ref[SUBLANES * (K_A - 1):, cs] = proj(C_CG) * proj(C_HA)
        fill_halo(eu_ref, cu_ref, hist_a_ref, K_A - 1, cs)
        acc = None
        for k in range(K_A):
            t = eu_ref[SUBLANES * k:SUBLANES * k + T, cs] * wca_ref[k:k + 1, cs]
            acc = t if acc is None else acc + t
        a_ref[:, cs] = (proj(C_BG) * acc).astype(BF16)

        ev_ref[SUBLANES * (K_B - 1):, cs] = proj(C_GA) * _sigmoid(proj(C_GB))
        fill_halo(ev_ref, cv_ref, hist_b_ref, K_B - 1, cs)
        for cc in range(MXU_COLS // LANES):
            col = slice(cs.start + cc * LANES, cs.start + (cc + 1) * LANES)
            wk = [jnp.broadcast_to(wcb_ref[k:k + 1, col], (ROWS, LANES)).astype(BF16)
                  for k in range(K_B)]
            n_win = 2 * (Q // 2 - 1) + K_B
            win = [ev_ref[SUBLANES * g:SUBLANES * g + ROWS, col].astype(BF16)
                   for g in range(n_win)]
            bias = jnp.broadcast_to(bcb_ref[:, col], (ROWS, LANES))
            for i in range(Q // 2):
                acc = bias
                for k0 in range(0, K_B, CONV_CHAIN):
                    run = None
                    for k in range(k0, min(k0 + CONV_CHAIN, K_B)):
                        t = win[2 * i + k] * wk[k]
                        run = t if run is None else run + t
                    acc = acc + run.astype(F32)
                cb_ref[i * ROWS:(i + 1) * ROWS, col] = acc

        for gi, c0 in enumerate((C_G0, C_G1, C_G2)):
            sg_ref[gi, :, cs] = _sigmoid(proj(c0))

        ep_ref[SUBLANES * POOL_PAD:, cs] = proj(C_PIN)
        fill_halo(ep_ref, cp_ref, hist_p_ref, POOL_PAD, cs)
        w = POOL_WINDOWS[n]
        cur = ep_ref[SUBLANES * POOL_PAD:, cs]
        acc = cur
        for s in range(1, w):
            acc = acc + ep_ref[SUBLANES * (POOL_PAD - s):SUBLANES * (POOL_PAD - s) + T, cs]
        if pos0 < POOL_PAD:
            inv = 1.0 / jnp.minimum(avail, w).astype(F32)
        else:
            inv = 1.0 / w
        c_ref[:, cs] = (acc * inv - cur).astype(BF16)

    for n in range(N_COLBLK):
        cs = _colblk(n)
        y_c = _dot(c_ref[:, cs], wpool_ref[n])
        m_ref[:, cs] = sg_ref[2, :, cs] * (y_c * pscale[:, cs])
    for n in range(N_COLBLK):
        cs = _colblk(n)
        m_ref[:, cs] += sg_ref[0, :, cs] * _dot(a_ref[...], woa_ref[:, cs])

    lng = lng_ref[...]
    lnb = lnb_ref[...]
    for i in range(n_blocks):
        rows = slice(i * ROWS, (i + 1) * ROWS)
        cb = cb_ref[rows, :]
        mu = jnp.mean(cb, axis=-1, keepdims=True)
        d = cb - mu
        var = jnp.mean(d * d, axis=-1, keepdims=True)
        y = d * lax.rsqrt(var + EPS) * lng + lnb
        b_ref[rows, :] = _silu(y).astype(BF16)

    for n in range(N_COLBLK):
        cs = _colblk(n)
        m = m_ref[:, cs] + sg_ref[1, :, cs] * _dot(b_ref[...], wob_ref[:, cs])
        h_ref[:, cs] = m.astype(BF16)

    for n in range(N_COLBLK):
        cs = _colblk(n)
        out = _dot(h_ref[...], wo_ref[:, cs])
        g1 = ada_ref[A_G1, :, cs]
        for i in range(n_blocks):
            rows = slice(i * ROWS, (i + 1) * ROWS)
            xo_ref[rows, cs] = xs_ref[rows, cs] + g1 * out[rows, :]

    if chained:
        @pl.when(j == pl.num_programs(1) - 1)
        def _():
            for carry_ref, s_ref, n_halo in ((cu_ref, sa_ref, K_A - 1),
                                             (cv_ref, sb_ref, K_B - 1),
                                             (cp_ref, sp_ref, POOL_PAD)):
                for g in range(n_halo):
                    s_ref[g:g + 1, :] = carry_ref[SUBLANES * g:SUBLANES * g + 1, :]
    else:
        for e_ref, s_ref, n_halo in ((eu_ref, sa_ref, K_A - 1),
                                     (ev_ref, sb_ref, K_B - 1),
                                     (ep_ref, sp_ref, POOL_PAD)):
            for s in range(SUBLANES // sps):
                r = sps * s + sps - 1
                for g in range(n_halo):
                    row = SUBLANES * (Q + g) + r
                    s_ref[s, g:g + 1, :] = e_ref[row:row + 1, :]


def _layer_spec(arr, l):
    nd = arr.ndim - 1
    return pl.BlockSpec((None,) + arr.shape[1:], lambda *_: (l,) + (0,) * nd,
                        pipeline_mode=pl.Buffered(1))


def _mixer_call(x, ada, hists, w, l, *, n_seq, tiles_per_seq, q_frames, sps, chained,
                permute_in, pos0):
    Q = q_frames
    T = SUBLANES * Q
    n_tiles = n_seq * tiles_per_seq
    tile_idx = lambda b, j: (b * tiles_per_seq + j, 0)
    in_specs = [pl.BlockSpec((T, D_MODEL), tile_idx)]
    args = [x]
    in_specs.append(pl.BlockSpec((None, 6, ROWS, D_MODEL), lambda b, j: (b, 0, 0, 0)))
    args.append(ada)
    if not chained:
        for hst in hists:
            in_specs.append(pl.BlockSpec((None,) + hst.shape[1:], lambda b, j: (b, 0, 0, 0)))
            args.append(hst)
    for arr in w:
        in_specs.append(_layer_spec(arr, l))
        args.append(arr)

    if chained:
        st_shape = lambda n: jax.ShapeDtypeStruct((n_seq, n, D_MODEL), F32)
        st_spec = lambda n: pl.BlockSpec((None, n, D_MODEL), lambda b, j: (b, 0, 0))
    else:
        spt = SUBLANES // sps
        st_shape = lambda n: jax.ShapeDtypeStruct((n_seq, spt, n, D_MODEL), F32)
        st_spec = lambda n: pl.BlockSpec((None, spt, n, D_MODEL), lambda b, j: (b, 0, 0, 0))
    halos = (K_A - 1, K_B - 1, POOL_PAD)
    out_shape = [jax.ShapeDtypeStruct((n_tiles * T, D_MODEL), F32)] + [st_shape(n) for n in halos]
    out_specs = [pl.BlockSpec((T, D_MODEL), tile_idx)] + [st_spec(n) for n in halos]

    scratch = [
        pltpu.VMEM((T, D_MODEL), BF16),
        pltpu.VMEM((3, T, D_MODEL), F32),
        pltpu.VMEM((T, D_MODEL), F32),
        pltpu.VMEM((T, D_MODEL), F32),
    ] + [pltpu.VMEM((SUBLANES * (n + Q), D_MODEL), F32) for n in halos] + [
        pltpu.VMEM((T, D_MODEL), BF16),
        pltpu.VMEM((T, D_MODEL), BF16),
        pltpu.VMEM((T, D_MODEL), BF16),
    ]
    if chained:
        scratch += [pltpu.VMEM((SUBLANES * n, D_MODEL), F32) for n in halos]
    if permute_in:
        scratch.append(pltpu.VMEM((T, D_MODEL), F32))

    return pl.pallas_call(
        functools.partial(_mixer_kernel, q_frames=Q, sps=sps, chained=chained,
                          permute_in=permute_in, pos0=pos0),
        grid=(n_seq, tiles_per_seq),
        in_specs=in_specs,
        out_specs=out_specs,
        out_shape=out_shape,
        scratch_shapes=scratch,
        compiler_params=pltpu.CompilerParams(
            dimension_semantics=("arbitrary", "arbitrary"),
            vmem_limit_bytes=VMEM_LIMIT_BYTES),
        name="mixer_chained" if chained else "mixer_streams",
    )(*args)


def _ffn_kernel(*refs, q_frames, final, n_sub):
    Q = q_frames
    TS = SUBLANES * Q
    T = n_sub * TS
    n_blocks = T // ROWS
    x_ref, ada_ref, n2g_ref, wfi_ref, wfo_ref, fg_ref, o_ref = refs[:7]
    h_ref, act_ref = refs[7:9]

    scale2 = n2g_ref[...] * (1.0 + ada_ref[A_S2])
    for i in range(n_blocks):
        rows = slice(i * ROWS, (i + 1) * ROWS)
        hrow = _rms(x_ref[rows, :]) * scale2 + ada_ref[A_SH2]
        h_ref[rows, :] = hrow.astype(BF16)

    h = h_ref[...]
    for f in range(D_FF // MXU_COLS):
        gate = _dot(h, wfi_ref[:, _colblk(f)])
        up = _dot(h, wfi_ref[:, _colblk(f, D_FF)])
        act_ref[:, _colblk(f)] = (_silu(gate) * up).astype(BF16)

    dst_ref = refs[9] if final else o_ref
    for n in range(N_COLBLK):
        cs = _colblk(n)
        y = _dot(act_ref[...], wfo_ref[:, cs])
        g2 = ada_ref[A_G2, :, cs]
        for i in range(n_blocks):
            rows = slice(i * ROWS, (i + 1) * ROWS)
            dst_ref[rows, cs] = x_ref[rows, cs] + g2 * y[rows, :]

    if final:
        xn_ref = refs[9]
        fg = fg_ref[...]
        for i in range(n_blocks):
            rows = slice(i * ROWS, (i + 1) * ROWS)
            xn_ref[rows, :] = _rms(xn_ref[rows, :]) * fg
        for sb in range(n_sub):
            blk = slice(sb * TS, (sb + 1) * TS)
            for c in range(N_CHUNKS):
                col = slice(c * LANES, (c + 1) * LANES)
                by_frame = xn_ref[blk, col].reshape(Q, SUBLANES, LANES)
                o_ref[blk, col] = jnp.swapaxes(by_frame, 0, 1).reshape(TS, LANES)


def _ffn_call(x, ada, n2g, wfi, wfo, fg, l, *, n_seq, tiles_per_seq, q_frames, final,
              n_sub):
    Q = q_frames
    T = n_sub * SUBLANES * Q
    assert tiles_per_seq % n_sub == 0
    tiles_per_seq = tiles_per_seq // n_sub
    n_tiles = n_seq * tiles_per_seq
    tile_idx = lambda b, j: (b * tiles_per_seq + j, 0)
    scratch = [
        pltpu.VMEM((T, D_MODEL), BF16),
        pltpu.VMEM((T, D_FF), BF16),
    ]
    if final:
        scratch.append(pltpu.VMEM((T, D_MODEL), F32))
    return pl.pallas_call(
        functools.partial(_ffn_kernel, q_frames=Q, final=final, n_sub=n_sub),
        grid=(n_seq, tiles_per_seq),
        in_specs=[
            pl.BlockSpec((T, D_MODEL), tile_idx),
            pl.BlockSpec((None, 6, ROWS, D_MODEL), lambda b, j: (b, 0, 0, 0)),
            _layer_spec(n2g, l), _layer_spec(wfi, l), _layer_spec(wfo, l),
            pl.BlockSpec(fg.shape, lambda b, j: (0, 0), pipeline_mode=pl.Buffered(1)),
        ],
        out_specs=pl.BlockSpec((T, D_MODEL), tile_idx),
        out_shape=jax.ShapeDtypeStruct((n_tiles * T, D_MODEL), F32),
        scratch_shapes=scratch,
        compiler_params=pltpu.CompilerParams(
            dimension_semantics=("arbitrary", "arbitrary"),
            vmem_limit_bytes=VMEM_LIMIT_BYTES),
        name="ffn_final" if final else "ffn",
    )(x, ada, n2g, wfi, wfo, fg)


def _run_group(x, ada, hists, w_mixer, w_ffn, final_g, *, n_seq, tiles_per_seq,
               q_frames, sps, chained, pos0):
    states = []
    for l in range(DEPTH):
        x, sa, sb, sp = _mixer_call(
            x, ada[l], None if chained else hists[l], w_mixer, l,
            n_seq=n_seq, tiles_per_seq=tiles_per_seq, q_frames=q_frames, sps=sps,
            chained=chained, permute_in=(l == 0), pos0=pos0)
        x = _ffn_call(x, ada[l], *w_ffn, final_g, l, n_seq=n_seq,
                      tiles_per_seq=tiles_per_seq, q_frames=q_frames,
                      final=(l == DEPTH - 1), n_sub=2 if chained else 1)
        states.append((sa, sb, sp))
    return x, states


def kernel(x_prompt, x_sample, c_prompt, c_sample, cache_conv_a, cache_conv_b, cache_pool, w_ada, b_ada, norm1_g, w_in, w_conv_a, w_out_a, w_conv_b, b_conv_b, ln_b_g, ln_b_b, w_out_b, w_pool, pool_scale, w_o, norm2_g, w_ffn_in, w_ffn_out, final_g):
    bp, sp_len, _ = x_prompt.shape
    bs, ss_len, _ = x_sample.shape
    q_frames = 64
    tile = SUBLANES * q_frames
    assert q_frames >= K_B - 1 and sp_len % tile == 0 and ss_len % q_frames == 0
    sps_s = ss_len // q_frames
    seq_per_tile = SUBLANES // sps_s
    assert SUBLANES % sps_s == 0 and bs % seq_per_tile == 0
    n_sgrp = bs // seq_per_tile

    row = lambda v: v.reshape(DEPTH, 1, -1)
    w_mixer = [row(norm1_g), w_in.astype(BF16), w_conv_a, w_out_a.astype(BF16),
               w_conv_b, row(b_conv_b), row(ln_b_g), row(ln_b_b),
               w_out_b.astype(BF16), w_pool.astype(BF16), row(pool_scale),
               w_o.astype(BF16)]
    w_ffn = [row(norm2_g), w_ffn_in.astype(BF16), w_ffn_out.astype(BF16)]
    fg = final_g.reshape(1, D_MODEL)

    ada = _ada_call(jnp.concatenate([c_prompt, c_sample], axis=0), w_ada, b_ada)
    ada = ada.reshape(DEPTH, bp + bs, 6, D_MODEL)
    ada_p = jnp.broadcast_to(ada[:, :bp, :, None, :], (DEPTH, bp, 6, ROWS, D_MODEL))
    ada_s = ada[:, bp:].reshape(DEPTH, n_sgrp, seq_per_tile, 6, D_MODEL)
    ada_s = jnp.repeat(jnp.transpose(ada_s, (0, 1, 3, 2, 4)), sps_s, axis=3)
    ada_s = jnp.tile(ada_s, (1, 1, 1, ROWS // SUBLANES, 1))

    def frame_major(cache):
        n = cache.shape[2]
        c = cache.reshape(DEPTH, n_sgrp, seq_per_tile, n, D_MODEL)
        return jnp.repeat(jnp.transpose(c, (0, 1, 3, 2, 4)), sps_s, axis=3)
    hists = list(zip(frame_major(cache_conv_a), frame_major(cache_conv_b),
                     frame_major(cache_pool)))

    yp, st_p = _run_group(
        x_prompt.reshape(bp * sp_len, D_MODEL), ada_p, None, w_mixer, w_ffn, fg,
        n_seq=bp, tiles_per_seq=sp_len // tile, q_frames=q_frames,
        sps=SUBLANES, chained=True, pos0=0)
    ys, st_s = _run_group(
        x_sample.reshape(bs * ss_len, D_MODEL), ada_s, hists, w_mixer, w_ffn, fg,
        n_seq=n_sgrp, tiles_per_seq=1, q_frames=q_frames, sps=sps_s, chained=False,
        pos0=PAST_LEN)

    def seq_major(s):
        return s.reshape(bs, s.shape[2], D_MODEL)

    return (yp.reshape(bp, sp_len, D_MODEL), ys.reshape(bs, ss_len, D_MODEL),
            jnp.stack([s[0] for s in st_p]), jnp.stack([s[1] for s in st_p]),
            jnp.stack([s[2] for s in st_p]),
            jnp.stack([seq_major(s[0]) for s in st_s]),
            jnp.stack([seq_major(s[1]) for s in st_s]),
            jnp.stack([seq_major(s[2]) for s in st_s]))
```
